```python
import jax, jax.numpy as jnp
from jax import lax
import numpy as np

D_MODEL = 1024
BATCH = 2
SEQ = 8192
DEPTH = 4

N_BRANCH = 4
BRANCH_WIDTH = D_MODEL // N_BRANCH
HEAD_DIM = 64
CONV_WIDTH = BRANCH_WIDTH
CONV_K = 3
RWKV_WIDTH = BRANCH_WIDTH
RWKV_HEADS = RWKV_WIDTH // HEAD_DIM
DECAY_LORA = 64
AAA_LORA = 64
GATE_LORA = 128
RWKV_LN_EPS = 64e-5
MLA_HEADS = BRANCH_WIDTH // HEAD_DIM
QK_NOPE = 64
QK_ROPE = 32
V_HEAD = 64
Q_LORA = 256
KV_LORA = 128
ROPE_THETA = 10000.0
Q_BLOCK = 128
FNET_WIDTH = BRANCH_WIDTH
FNET_GROUPS = FNET_WIDTH // HEAD_DIM
D_FF = -(-8 * D_MODEL // (3 * 256)) * 256
NORM_EPS = 1e-6
IN_SIZES = (CONV_WIDTH, CONV_WIDTH, CONV_WIDTH,
            RWKV_WIDTH, RWKV_WIDTH, RWKV_WIDTH,
            2 * DECAY_LORA, 2 * AAA_LORA, GATE_LORA,
            Q_LORA, KV_LORA + QK_ROPE,
            FNET_WIDTH,
            N_BRANCH * D_MODEL)
IN_WIDTH = sum(IN_SIZES)

kernel_name = "hybrid_conv_rwkv7_mla_fnet_encoder"


def _rmsnorm(x, g, eps=NORM_EPS):
    xf = x.astype(jnp.float32)
    y = xf * lax.rsqrt(jnp.mean(xf * xf, axis=-1, keepdims=True) + eps)
    return (y * g.astype(jnp.float32)).astype(x.dtype)


def _rope(x, cos, sin):
    x1, x2 = jnp.split(x, 2, axis=-1)
    return jnp.concatenate([x1 * cos - x2 * sin, x2 * cos + x1 * sin], axis=-1).astype(x.dtype)


def _short_conv_mixer(u, b, c, conv_w, w_out):
    z = c * u
    zp = jnp.pad(z, ((0, 0), (1, 1), (0, 0)))
    y = zp[:, :-2] * conv_w[0] + zp[:, 1:-1] * conv_w[1] + zp[:, 2:] * conv_w[2]
    return (b * y) @ w_out


def _rwkv7_mixer(r, k, v, w_lo, a_lo, g_lo, mu, w0, w_up, a0, a_up, g_up,
                 k_k, k_a, r_k, ln_g, ln_b, w_out):
    f32 = jnp.float32
    Bn, S, W = r.shape
    H, K = RWKV_HEADS, HEAD_DIM
    prev = lambda t: jnp.pad(t, ((0, 0), (1, 0), (0, 0)))[:, :-1]
    nxt = lambda t: jnp.pad(t, ((0, 0), (0, 1), (0, 0)))[:, 1:]

    def shifted(t, i):
        return jnp.stack([t + mu[0, i] * (prev(t) - t), t + mu[1, i] * (nxt(t) - t)])

    rd, kd, vd = shifted(r, 0), shifted(k, 1), shifted(v, 2)
    w_l = jnp.tanh(w_lo.reshape(Bn, S, 2, DECAY_LORA))
    a_l = a_lo.reshape(Bn, S, 2, AAA_LORA)
    w_log = -jax.nn.softplus(-(w0[:, None, None, :] + jnp.einsum('bsdr,drc->dbsc', w_l, w_up)).astype(f32)) - 0.5
    decay = jnp.exp(-jnp.exp(w_log))
    a = jax.nn.sigmoid((a0[:, None, None, :] + jnp.einsum('bsdr,drc->dbsc', a_l, a_up)).astype(f32))
    g = jax.nn.sigmoid(g_lo) @ g_up
    heads = lambda t: t.astype(f32).reshape(2, Bn, S, H, K)
    rd, kd, vd, decay, a = heads(rd), heads(kd), heads(vd), heads(decay), heads(a)
    kk = kd * k_k.astype(f32).reshape(H, K)
    kk = kk / jnp.maximum(jnp.linalg.norm(kk, axis=-1, keepdims=True), 1e-12)
    kt = kd * (1.0 + (a - 1.0) * k_a.astype(f32).reshape(H, K))

    def to_time(t):
        t = jnp.stack([t[0], jnp.flip(t[1], axis=1)])
        return jnp.moveaxis(t, 2, 0)

    def step(state, inp):
        r_t, w_t, k_t, v_t, kk_t, a_t = inp
        sa = jnp.einsum('dbhvk,dbhk->dbhv', state, kk_t)
        state = (state * w_t[..., None, :] - sa[..., None] * (kk_t * a_t)[..., None, :]
                 + v_t[..., :, None] * k_t[..., None, :])
        return state, jnp.einsum('dbhvk,dbhk->dbhv', state, r_t)

    s0 = jnp.zeros((2, Bn, H, K, K), f32)
    _, ys = lax.scan(step, s0, (to_time(rd), to_time(decay), to_time(kt),
                               to_time(vd), to_time(kk), to_time(a)))
    ys = jnp.moveaxis(ys, 0, 2)
    y = ys[0] + jnp.flip(ys[1], axis=1)
    mean = jnp.mean(y, axis=-1, keepdims=True)
    var = jnp.mean(jnp.square(y - mean), axis=-1, keepdims=True)
    y = (y - mean) * lax.rsqrt(var + RWKV_LN_EPS)
    y = y * ln_g.astype(f32).reshape(H, K) + ln_b.astype(f32).reshape(H, K)
    bonus = jnp.sum(jnp.sum(rd * kt * r_k.astype(f32), axis=-1, keepdims=True) * vd, axis=0)
    out = (y + bonus).reshape(Bn, S, W).astype(r.dtype) * g
    return out @ w_out


def _mla_mixer(q_lo, kv_lo, cos, sin, q_norm, w_uq, kv_norm, w_ukv, w_out):
    Bn, S, _ = q_lo.shape
    H = MLA_HEADS
    q = (_rmsnorm(q_lo, q_norm) @ w_uq).reshape(Bn, S, H, QK_NOPE + QK_ROPE)
    q_nope = q[..., :QK_NOPE]
    q_rope = _rope(q[..., QK_NOPE:], cos[:, :, None, :], sin[:, :, None, :])
    k_rope = _rope(kv_lo[..., KV_LORA:], cos, sin)
    kv = (_rmsnorm(kv_lo[..., :KV_LORA], kv_norm) @ w_ukv).reshape(Bn, S, H, QK_NOPE + V_HEAD)
    k_nope, v = kv[..., :QK_NOPE], kv[..., QK_NOPE:]
    scale = (QK_NOPE + QK_ROPE) ** -0.5
    nblk = S // Q_BLOCK
    blocks = lambda t: jnp.moveaxis(t.reshape(Bn, nblk, Q_BLOCK, H, t.shape[-1]), 1, 0)

    def attend(qb):
        qn, qr = qb
        s = (jnp.einsum('bqhd,bkhd->bhqk', qn, k_nope)
             + jnp.einsum('bqhd,bkd->bhqk', qr, k_rope))
        p = jax.nn.softmax(s.astype(jnp.float32) * scale, axis=-1).astype(v.dtype)
        return jnp.einsum('bhqk,bkhd->bqhd', p, v)

    o = lax.map(attend, (blocks(q_nope), blocks(q_rope)))
    o = jnp.moveaxis(o, 0, 1).reshape(Bn, S, H * V_HEAD)
    return o @ w_out


def _fourier_mixer(u, w_out):
    Bn, S, _ = u.shape
    z = u.reshape(Bn, S, FNET_GROUPS, FNET_WIDTH // FNET_GROUPS).astype(jnp.float32)
    y = jnp.fft.fft2(z, axes=(1, 3), norm="ortho").real
    return y.reshape(Bn, S, FNET_WIDTH).astype(u.dtype) @ w_out


def setup_inputs(seed: int = 0) -> dict:
    key = jax.random.key(seed)
    ks = iter(jax.random.split(key, 40))
    L = DEPTH
    nrm = lambda shape, s: jax.random.normal(next(ks), shape, jnp.float32) * s
    gain = lambda shape: 1.0 + nrm(shape, 0.02)
    x = jax.random.normal(next(ks), (BATCH, SEQ, D_MODEL), jnp.float32)
    positions = (jnp.arange(SEQ, dtype=jnp.int32)[None, :]
                 + jax.random.randint(next(ks), (BATCH, 1), 0, 1024, dtype=jnp.int32))
    return {
        "x": x,
        "positions": positions,
        "mix_norm": gain((L, D_MODEL)),
        "w_in": nrm((L, D_MODEL, IN_WIDTH), D_MODEL ** -0.5),
        "gate_bias": nrm((L, N_BRANCH, D_MODEL), 0.1),
        "conv_w": nrm((L, CONV_K, CONV_WIDTH), CONV_K ** -0.5),
        "conv_out": nrm((L, CONV_WIDTH, D_MODEL), CONV_WIDTH ** -0.5),
        "rwkv_mu": jax.random.uniform(next(ks), (L, 2, 3, RWKV_WIDTH), jnp.float32),
        "rwkv_w0": nrm((L, 2, RWKV_WIDTH), 0.5) - 0.5,
        "rwkv_w_up": nrm((L, 2, DECAY_LORA, RWKV_WIDTH), 0.1),
        "rwkv_a0": nrm((L, 2, RWKV_WIDTH), 0.1),
        "rwkv_a_up": nrm((L, 2, AAA_LORA, RWKV_WIDTH), 0.1),
        "rwkv_g_up": nrm((L, GATE_LORA, RWKV_WIDTH), GATE_LORA ** -0.5),
        "rwkv_k_k": 0.85 + nrm((L, RWKV_WIDTH), 0.05),
        "rwkv_k_a": 1.0 + nrm((L, RWKV_WIDTH), 0.05),
        "rwkv_r_k": nrm((L, RWKV_HEADS, HEAD_DIM), 0.1),
        "rwkv_ln_g": gain((L, RWKV_WIDTH)),
        "rwkv_ln_b": nrm((L, RWKV_WIDTH), 0.02),
        "rwkv_out": nrm((L, RWKV_WIDTH, D_MODEL), RWKV_WIDTH ** -0.5),
        "mla_q_norm": gain((L, Q_LORA)),
        "mla_w_uq": nrm((L, Q_LORA, MLA_HEADS * (QK_NOPE + QK_ROPE)), Q_LORA ** -0.5),
        "mla_kv_norm": gain((L, KV_LORA)),
        "mla_w_ukv": nrm((L, KV_LORA, MLA_HEADS * (QK_NOPE + V_HEAD)), KV_LORA ** -0.5),
        "mla_out": nrm((L, MLA_HEADS * V_HEAD, D_MODEL), (MLA_HEADS * V_HEAD) ** -0.5),
        "fnet_out": nrm((L, FNET_WIDTH, D_MODEL), FNET_WIDTH ** -0.5),
        "w_o": nrm((L, D_MODEL, D_MODEL), D_MODEL ** -0.5),
        "ffn_norm": gain((L, D_MODEL)),
        "ffn_w_gu": nrm((L, D_MODEL, 2 * D_FF), D_MODEL ** -0.5),
        "ffn_w_down": nrm((L, D_FF, D_MODEL), D_FF ** -0.5),
        "final_norm": gain((D_MODEL,)),
    }


def reference(x, positions, mix_norm, w_in, gate_bias, conv_w, conv_out,
              rwkv_mu, rwkv_w0, rwkv_w_up, rwkv_a0, rwkv_a_up, rwkv_g_up,
              rwkv_k_k, rwkv_k_a, rwkv_r_k, rwkv_ln_g, rwkv_ln_b, rwkv_out,
              mla_q_norm, mla_w_uq, mla_kv_norm, mla_w_ukv, mla_out,
              fnet_out, w_o, ffn_norm, ffn_w_gu, ffn_w_down, final_norm):
    Bn, S, D = x.shape
    inv_freq = ROPE_THETA ** (-jnp.arange(0, QK_ROPE, 2, dtype=jnp.float32) / QK_ROPE)
    ang = positions.astype(jnp.float32)[..., None] * inv_freq
    cos, sin = jnp.cos(ang), jnp.sin(ang)
    cuts = [int(c) for c in np.cumsum(IN_SIZES)[:-1]]
    for l in range(DEPTH):
        h = _rmsnorm(x, mix_norm[l])
        (c_x, c_b, c_c, r, k, v, w_lo, a_lo, g_lo,
         q_lo, kv_lo, f_in, gate_logits) = jnp.split(h @ w_in[l], cuts, axis=-1)
        y_a = _short_conv_mixer(c_x, c_b, c_c, conv_w[l], conv_out[l])
        y_b = _rwkv7_mixer(r, k, v, w_lo, a_lo, g_lo, rwkv_mu[l], rwkv_w0[l], rwkv_w_up[l],
                           rwkv_a0[l], rwkv_a_up[l], rwkv_g_up[l], rwkv_k_k[l], rwkv_k_a[l],
                           rwkv_r_k[l], rwkv_ln_g[l], rwkv_ln_b[l], rwkv_out[l])
        y_c = _mla_mixer(q_lo, kv_lo, cos, sin, mla_q_norm[l], mla_w_uq[l],
                         mla_kv_norm[l], mla_w_ukv[l], mla_out[l])
        y_d = _fourier_mixer(f_in, fnet_out[l])
        gates = jax.nn.sigmoid((gate_logits.reshape(Bn, S, N_BRANCH, D) + gate_bias[l])
                               .astype(jnp.float32)).astype(x.dtype)
        branches = jnp.stack([y_a, y_b, y_c, y_d], axis=2)
        x = x + jnp.sum(gates * branches, axis=2) @ w_o[l]
        h2 = _rmsnorm(x, ffn_norm[l])
        gt, up = jnp.split(h2 @ ffn_w_gu[l], 2, axis=-1)
        x = x + (jax.nn.silu(gt) * up) @ ffn_w_down[l]
    return _rmsnorm(x, final_norm)
```

```python
import functools

import numpy as np
import jax
import jax.numpy as jnp
from jax import lax
from jax.experimental import pallas as pl
from jax.experimental.pallas import tpu as pltpu

F32 = jnp.float32
BF16 = jnp.bfloat16
HI = lax.Precision.HIGHEST

D_MODEL = 1024
N_BRANCH = 4
BW = 256
HD = 64
NH = BW // HD
LORA = 64
GATE_LORA = 128
Q_LORA = 256
KV_LORA = 128
QK_NOPE = 64
QK_ROPE = 32
D_FF = 2816
NORM_EPS = 1e-6
RWKV_LN_EPS = 64e-5
ROPE_THETA = 10000.0
CHUNK = 64
FFT_N2 = 128

P_CONV = 0
P_RKV = 768
P_LORA = 1536
P_Q = 2048
P_KV = 2304
P_F = 2560
P_W = 2816

VMEM_LIMIT = 56 * 1024 * 1024


def _cparams(sem):
    return pltpu.CompilerParams(dimension_semantics=sem, vmem_limit_bytes=VMEM_LIMIT)


def _dot(a, b, prec=None):
    return jnp.dot(a, b, preferred_element_type=F32, precision=prec)


def _dot_nt(a, b, prec=None):
    return lax.dot_general(a, b, (((1,), (1,)), ((), ())),
                           preferred_element_type=F32, precision=prec)


def _rms(xf, g, eps=NORM_EPS):
    return xf * lax.rsqrt(jnp.mean(xf * xf, axis=-1, keepdims=True) + eps) * g


def _sigmoid(z):
    return 1.0 / (1.0 + jnp.exp(-z))


def _norm_mm_kernel(x_ref, g_ref, w_ref, b_ref, o_ref, h_ref, *, gate):
    @pl.when(pl.program_id(1) == 0)
    def _():
        h_ref[...] = _rms(x_ref[...], g_ref[...]).astype(BF16)

    acc = _dot(h_ref[...], w_ref[...])
    if gate:
        acc = _sigmoid(acc + b_ref[...])
    o_ref[...] = acc.astype(o_ref.dtype)


def _norm_mm(x, g, w, b, *, tm, tn, gate, out_dtype):
    n, d = x.shape
    nc = w.shape[1]
    return pl.pallas_call(
        functools.partial(_norm_mm_kernel, gate=gate),
        grid=(n // tm, nc // tn),
        in_specs=[
            pl.BlockSpec((tm, d), lambda i, j: (i, 0)),
            pl.BlockSpec((1, d), lambda i, j: (0, 0)),
            pl.BlockSpec((d, tn), lambda i, j: (0, j)),
            pl.BlockSpec((1, tn), lambda i, j: (0, j)),
        ],
        out_specs=pl.BlockSpec((tm, tn), lambda i, j: (i, j)),
        out_shape=jax.ShapeDtypeStruct((n, nc), out_dtype),
        scratch_shapes=[pltpu.VMEM((tm, d), BF16)],
        compiler_params=_cparams(("parallel", "arbitrary")),
        name="norm_mm_gate" if gate else "norm_mm",
    )(x, g, w, b)


def _shifted(t, prev_blk, next_blk, i, n_i):
    ts = t.shape[0]
    prev_row = jnp.where(i == 0, 0.0, prev_blk[7:8, :])
    next_row = jnp.where(i == n_i - 1, 0.0, next_blk[0:1, :])
    rows = lax.broadcasted_iota(jnp.int32, (ts, 1), 0)
    t_prev = jnp.where(rows == 0, prev_row, pltpu.roll(t, 1, axis=0))
    t_next = jnp.where(rows == ts - 1, next_row, pltpu.roll(t, ts - 1, axis=0))
    return t_prev, t_next


def _halo_specs(ts, width, col_blk, s_len):
    r8 = ts // 8
    last8 = s_len // 8 - 1
    main = pl.BlockSpec((None, ts, width), lambda b, i: (b, i, col_blk))
    prev = pl.BlockSpec((None, 8, width),
                        lambda b, i: (b, jnp.maximum(i * r8 - 1, 0), col_blk))
    nxt = pl.BlockSpec((None, 8, width),
                       lambda b, i: (b, jnp.minimum((i + 1) * r8, last8), col_blk))
    return main, prev, nxt


def _softplus(z):
    return jnp.maximum(z, 0.0) + jnp.log(1.0 + jnp.exp(-jnp.abs(z)))


def _rwkv_prep_kernel(rkv_ref, prev_ref, next_ref, lora_ref, mu_ref, w0_ref, wup_ref,
                      a0_ref, aup_ref, gup_ref, kk_ref, ka_ref, rk_ref, bd_ref,
                      r_out, lw_out, kt_out, v_out, kkn_out, b_out, bonus_out, g_out):
    i = pl.program_id(1)
    n_i = pl.num_programs(1)
    t = rkv_ref[...]
    t_prev, t_next = _shifted(t, prev_ref[...], next_ref[...], i, n_i)
    lora = lora_ref[...]
    bd = bd_ref[...]
    bonus = None
    for d in range(2):
        sh = t_prev if d == 0 else t_next
        mixed = t + mu_ref[d:d + 1, :] * (sh - t)
        rd = mixed[:, 0:BW]
        kd = mixed[:, BW:2 * BW]
        vd = mixed[:, 2 * BW:3 * BW]
        w_l = jnp.tanh(lora[:, d * LORA:(d + 1) * LORA])
        a_l = lora[:, 2 * LORA + d * LORA:2 * LORA + (d + 1) * LORA]
        w_pre = w0_ref[d:d + 1, :] + _dot(w_l, wup_ref[d], HI)
        w_log = -_softplus(-w_pre) - 0.5
        lw = -jnp.exp(w_log)
        a = _sigmoid(a0_ref[d:d + 1, :] + _dot(a_l, aup_ref[d], HI))
        kk = kd * kk_ref[...]
        ss = _dot(kk * kk, bd, HI)
        kk = kk / jnp.maximum(jnp.sqrt(ss), 1e-12)
        kt = kd * (1.0 + (a - 1.0) * ka_ref[...])
        r_out[d] = rd
        lw_out[d] = lw
        kt_out[d] = kt
        v_out[d] = vd
        kkn_out[d] = kk
        b_out[d] = kk * a
        bo = _dot(rd * kt * rk_ref[...], bd, HI) * vd
        bonus = bo if bonus is None else bonus + bo
    bonus_out[...] = bonus
    g_out[...] = _dot(_sigmoid(lora[:, 4 * LORA:4 * LORA + GATE_LORA]), gup_ref[...], HI)


def _rwkv_prep(p3, mu, w0, wup, a0, aup, gup, k_k, k_a, r_k, bd, *, ts):
    bsz, s_len, _ = p3.shape
    main, prev, nxt = _halo_specs(ts, 3 * BW, P_RKV // (3 * BW), s_len)
    lora = pl.BlockSpec((None, ts, 384), lambda b, i: (b, i, P_LORA // 384))
    full = lambda shape: pl.BlockSpec(shape, lambda b, i: (0,) * len(shape))
    dir_out = pl.BlockSpec((2, None, ts, BW), lambda b, i: (0, b, i, 0))
    one_out = pl.BlockSpec((None, ts, BW), lambda b, i: (b, i, 0))
    dir_shape = jax.ShapeDtypeStruct((2, bsz, s_len, BW), F32)
    one_shape = jax.ShapeDtypeStruct((bsz, s_len, BW), F32)
    return pl.pallas_call(
        _rwkv_prep_kernel,
        grid=(bsz, s_len // ts),
        in_specs=[main, prev, nxt, lora,
                  full((2, 3 * BW)), full((2, BW)), full((2, LORA, BW)),
                  full((2, BW)), full((2, LORA, BW)), full((GATE_LORA, BW)),
                  full((1, BW)), full((1, BW)), full((1, BW)), full((BW, BW))],
        out_specs=[dir_out] * 6 + [one_out] * 2,
        out_shape=[dir_shape] * 6 + [one_shape] * 2,
        compiler_params=_cparams(("parallel", "parallel")),
        name="rwkv_prep",
    )(p3, p3, p3, p3, mu, w0, wup, a0, aup, gup, k_k, k_a, r_k, bd)


def _rwkv_scan_kernel(r_ref, lw_ref, kt_ref, v_ref, kk_ref, b_ref, y_ref, h_ref, *, n_chunk):
    d = pl.program_id(0)
    sgn = 1 - 2 * d

    @pl.when(pl.program_id(2) == 0)
    def _():
        h_ref[...] = jnp.zeros_like(h_ref)

    rows = lax.broadcasted_iota(jnp.int32, (CHUNK, CHUNK), 0)
    cols = lax.broadcasted_iota(jnp.int32, (CHUNK, CHUNK), 1)
    diff = (rows - cols) * sgn
    strict = diff > 0
    incl = diff >= 0
    tri = jnp.where(incl, 1.0, 0.0).astype(F32)
    eye = jnp.where(diff == 0, 1.0, 0.0).astype(F32)

    def chunk_body(c, carry):
        cidx = c + d * (n_chunk - 1 - 2 * c)
        off = pl.multiple_of(cidx * CHUNK, CHUNK)
        sl = pl.ds(off, CHUNK)
        lw = lw_ref[sl, :]
        cum = _dot(tri, lw, HI)
        g_excl = jnp.exp(cum - lw)
        g_incl = jnp.exp(cum)
        g_inv = jnp.exp(-cum)
        g_tot = jnp.exp(jnp.sum(lw, axis=0, keepdims=True))
        at = -kk_ref[sl, :] * g_excl
        rt = r_ref[sl, :] * g_incl
        bt = b_ref[sl, :] * g_inv
        kt = kt_ref[sl, :] * g_inv
        v = v_ref[sl, :]
        ys = []
        for h in range(NH):
            hs = slice(h * HD, (h + 1) * HD)
            at_h, rt_h, bt_h, kt_h, v_h = at[:, hs], rt[:, hs], bt[:, hs], kt[:, hs], v[:, hs]
            m = _dot_nt(jnp.concatenate([at_h, rt_h], axis=0),
                        jnp.concatenate([bt_h, kt_h], axis=0), HI)
            a_ab = jnp.where(strict, m[:CHUNK, :CHUNK], 0.0)
            a_ak = jnp.where(strict, m[:CHUNK, CHUNK:], 0.0)
            a_rb = jnp.where(incl, m[CHUNK:, :CHUNK], 0.0)
            a_rk = jnp.where(incl, m[CHUNK:, CHUNK:], 0.0)
            tm = eye + a_ab
            pw = a_ab
            for _ in range(5):
                pw = _dot(pw, pw, HI)
                tm = tm + _dot(tm, pw, HI)
            pq = _dot(tm, jnp.concatenate([at_h, _dot(a_ak, v_h, HI)], axis=1), HI)
            gy = (jnp.concatenate([rt_h, _dot(a_rk, v_h, HI)], axis=1)
                  + _dot(a_rb, pq, HI))
            hst = h_ref[h]
            gp = _dot_nt(jnp.concatenate([gy[:, :HD], pq[:, :HD]], axis=0), hst, HI)
            ys.append(gp[:CHUNK] + gy[:, HD:])
            u = gp[CHUNK:] + pq[:, HD:]
            upd = _dot(jnp.concatenate([u, v_h], axis=0).T,
                       jnp.concatenate([bt_h, kt_h], axis=0), HI)
            h_ref[h] = (hst + upd) * g_tot[:, hs]
        y_ref[sl, :] = jnp.concatenate(ys, axis=1)
        return carry

    lax.fori_loop(0, n_chunk, chunk_body, 0)


def _rwkv_scan(r, lw, kt, v, kk, b, *, tb):
    _, bsz, s_len, _ = r.shape
    nb = s_len // tb
    spec = pl.BlockSpec((None, None, tb, BW),
                        lambda d, bb, i: (d, bb, i + d * (nb - 1 - 2 * i), 0))
    return pl.pallas_call(
        functools.partial(_rwkv_scan_kernel, n_chunk=tb // CHUNK),
        grid=(2, bsz, nb),
        in_specs=[spec] * 6,
        out_specs=spec,
        out_shape=jax.ShapeDtypeStruct(r.shape, F32),
        scratch_shapes=[pltpu.VMEM((NH, HD, HD), F32)],
        compiler_params=_cparams(("parallel", "parallel", "arbitrary")),
        name="rwkv_scan",
    )(r, lw, kt, v, kk, b)


def _mla_prep_kernel(qlo_ref, kvlo_ref, cs_ref, sn_ref, qn_ref, wq_ref, wqs_ref,
                     kvn_ref, wk_ref, wv_ref, pl_ref, pls_ref, q_out, k_out, v_out, *, scale):
    cs = jnp.concatenate([cs_ref[...]] * NH, axis=1)
    sn = jnp.concatenate([sn_ref[...]] * NH, axis=1)
    hq = _rms(qlo_ref[...], qn_ref[...]).astype(BF16)
    q = _dot(hq, wq_ref[...]) * cs + _dot(hq, wqs_ref[...]) * sn
    q_out[...] = (q * scale).astype(BF16)
    kv = kvlo_ref[...]
    hkv = _rms(kv[:, :KV_LORA], kvn_ref[...]).astype(BF16)
    kr = kv[:, KV_LORA:]
    k = (_dot(hkv, wk_ref[...]) + _dot(kr, pl_ref[...], HI)) * cs + _dot(kr, pls_ref[...], HI) * sn
    k_out[...] = k.astype(BF16)
    v_out[...] = _dot(hkv, wv_ref[...]).astype(BF16)


def _mla_prep(p2, cs, sn, qn, wq, wqs, kvn, wk, wv, plc, pls, *, ts):
    n = p2.shape[0]
    full = lambda shape: pl.BlockSpec(shape, lambda i: (0,) * len(shape))
    hw = NH * 128
    out = pl.BlockSpec((ts, hw), lambda i: (i, 0))
    shape = jax.ShapeDtypeStruct((n, hw), BF16)
    return pl.pallas_call(
        functools.partial(_mla_prep_kernel, scale=float((QK_NOPE + QK_ROPE) ** -0.5)),
        grid=(n // ts,),
        in_specs=[pl.BlockSpec((ts, 256), lambda i: (i, P_Q // 256)),
                  pl.BlockSpec((ts, 256), lambda i: (i, P_KV // 256)),
                  pl.BlockSpec((ts, 128), lambda i: (i, 0)),
                  pl.BlockSpec((ts, 128), lambda i: (i, 0)),
                  full((1, Q_LORA)), full((Q_LORA, hw)), full((Q_LORA, hw)),
                  full((1, KV_LORA)), full((KV_LORA, hw)), full((KV_LORA, hw)),
                  full((128, hw)), full((128, hw))],
        out_specs=[out] * 3,
        out_shape=[shape] * 3,
        compiler_params=_cparams(("parallel",)),
        name="mla_prep",
    )(p2, p2, cs, sn, qn, wq, wqs, kvn, wk, wv, plc, pls)


def _attn_kernel(q_ref, k_ref, v_ref, o_ref, m_ref, l_ref, acc_ref):
    j = pl.program_id(2)

    @pl.when(j == 0)
    def _():
        m_ref[...] = jnp.full_like(m_ref, -jnp.inf)
        l_ref[...] = jnp.zeros_like(l_ref)
        acc_ref[...] = jnp.zeros_like(acc_ref)

    for h in range(NH):
        hs = slice(h * 128, (h + 1) * 128)
        s = _dot_nt(q_ref[:, hs], k_ref[:, hs])
        m_old = m_ref[h]
        m_new = jnp.maximum(m_old, jnp.max(s, axis=-1, keepdims=True))
        p = jnp.exp(s - m_new)
        alpha = jnp.exp(m_old - m_new)
        l_ref[h] = alpha * l_ref[h] + jnp.sum(p, axis=-1, keepdims=True)
        acc_ref[h] = alpha * acc_ref[h] + _dot(p.astype(BF16), v_ref[:, hs])
        m_ref[h] = m_new

    @pl.when(j == pl.num_programs(2) - 1)
    def _():
        for h in range(NH):
            o_ref[:, h * 128:(h + 1) * 128] = (acc_ref[h] / l_ref[h]).astype(o_ref.dtype)


def _attention(q, k, v, *, tq, tk):
    bsz, s_len, hw = q.shape
    return pl.pallas_call(
        _attn_kernel,
        grid=(bsz, s_len // tq, s_len // tk),
        in_specs=[pl.BlockSpec((None, tq, hw), lambda b, i, j: (b, i, 0)),
                  pl.BlockSpec((None, tk, hw), lambda b, i, j: (b, j, 0)),
                  pl.BlockSpec((None, tk, hw), lambda b, i, j: (b, j, 0))],
        out_specs=pl.BlockSpec((None, tq, hw), lambda b, i, j: (b, i, 0)),
        out_shape=jax.ShapeDtypeStruct((bsz, s_len, hw), BF16),
        scratch_shapes=[pltpu.VMEM((NH, tq, 1), F32), pltpu.VMEM((NH, tq, 1), F32),
                        pltpu.VMEM((NH, tq, 128), F32)],
        compiler_params=_cparams(("parallel", "parallel", "arbitrary")),
        name="mla_attention",
    )(q, k, v)


def _fft1_kernel(x_ref, c1_ref, s1_ref, tc_ref, ts_ref, tr_out, ti_out):
    x = x_ref[...]
    ar = _dot(c1_ref[...], x, HI)
    ai = -_dot(s1_ref[...], x, HI)
    tc = tc_ref[...]
    tsn = ts_ref[...]
    tr_out[...] = ar * tc + ai * tsn
    ti_out[...] = ai * tc - ar * tsn


def _fft1(xf, c1, s1, twc, tws, *, wb):
    bsz, n1, wide = xf.shape
    blk = pl.BlockSpec((None, n1, wb), lambda b, j: (b, 0, j))
    tw = pl.BlockSpec((n1, wb), lambda b, j: (0, j))
    mat = pl.BlockSpec((n1, n1), lambda b, j: (0, 0))
    shape = jax.ShapeDtypeStruct((bsz, n1, wide), F32)
    return pl.pallas_call(
        _fft1_kernel,
        grid=(bsz, wide // wb),
        in_specs=[blk, mat, mat, tw, tw],
        out_specs=[blk, blk],
        out_shape=[shape, shape],
        compiler_params=_cparams(("parallel", "parallel")),
        name="fft_stage1",
    )(xf, c1, s1, twc, tws)


def _fft2_kernel(tr_ref, ti_ref, c2_ref, s2_ref, cg_ref, sg_ref, o_ref, *, kb):
    c2 = c2_ref[...]
    s2 = s2_ref[...]
    for q in range(kb):
        tr = tr_ref[q]
        ti = ti_ref[q]
        ur = _dot(c2, tr, HI) + _dot(s2, ti, HI)
        ui = _dot(c2, ti, HI) - _dot(s2, tr, HI)
        o_ref[:, q * BW:(q + 1) * BW] = _dot(ur, cg_ref[...], HI) + _dot(ui, sg_ref[...], HI)


def _fft2(tr, ti, c2, s2, cg, sg, *, kb):
    bsz, n1, n2, _ = tr.shape
    blk = pl.BlockSpec((None, kb, n2, BW), lambda b, j: (b, j, 0, 0))
    full = lambda shape: pl.BlockSpec(shape, lambda b, j: (0,) * len(shape))
    return pl.pallas_call(
        functools.partial(_fft2_kernel, kb=kb),
        grid=(bsz, n1 // kb),
        in_specs=[blk, blk, full((n2, n2)), full((n2, n2)), full((BW, BW)), full((BW, BW))],
        out_specs=pl.BlockSpec((None, n2, kb * BW), lambda b, j: (b, 0, j)),
        out_shape=jax.ShapeDtypeStruct((bsz, n2, n1 * BW), F32),
        compiler_params=_cparams(("parallel", "parallel")),
        name="fft_stage2",
    )(tr, ti, c2, s2, cg, sg)


def _dft_mats(n):
    idx = np.arange(n)
    ang = 2.0 * np.pi * ((idx[:, None] * idx[None, :]) % n) / n
    return np.cos(ang), np.sin(ang)


def _fourier_constants(s_len):
    n2 = FFT_N2
    n1 = s_len // n2
    c1, s1 = _dft_mats(n1)
    c2, s2 = _dft_mats(n2)
    k1 = np.arange(n1)[:, None]
    m2 = np.arange(n2)[None, :]
    ang = 2.0 * np.pi * ((k1 * m2) % s_len) / s_len
    twc, tws = np.cos(ang), np.sin(ang)
    cg64, sg64 = _dft_mats(HD)
    norm = 1.0 / np.sqrt(float(s_len) * HD)
    cg = np.kron(np.eye(NH), cg64) * norm
    sg = np.kron(np.eye(NH), sg64) * norm
    f = lambda a: jnp.asarray(a, F32)
    twc = jnp.repeat(f(twc), BW, axis=1)
    tws = jnp.repeat(f(tws), BW, axis=1)
    return f(c1), f(s1), twc, tws, f(c2), f(s2), f(cg), f(sg)


def _combine_kernel(x_ref, cv_ref, cvp_ref, cvn_ref, yf_ref, yb_ref, bonus_ref, rg_ref,
                    mla_ref, fn_ref, gates_ref, cw_ref, lng_ref, lnb_ref, avg_ref,
                    wa_ref, wb_ref, wc_ref, wd_ref, wo_ref, o_ref):
    i = pl.program_id(1)
    n_i = pl.num_programs(1)
    cv = cv_ref[...]
    cvp = cvp_ref[...]
    cvn = cvn_ref[...]
    z = cv[:, 2 * BW:] * cv[:, :BW]
    zp, zn = _shifted(z, cvp[:, 2 * BW:] * cvp[:, :BW], cvn[:, 2 * BW:] * cvn[:, :BW], i, n_i)
    conv = zp * cw_ref[0:1, :] + z * cw_ref[1:2, :] + zn * cw_ref[2:3, :]
    ya = _dot((cv[:, BW:2 * BW] * conv).astype(BF16), wa_ref[...])
    y = yf_ref[...] + yb_ref[...]
    avg = avg_ref[...]
    mean = _dot(y, avg, HI)
    yc = y - mean
    var = _dot(yc * yc, avg, HI)
    yn = yc * lax.rsqrt(var + RWKV_LN_EPS) * lng_ref[...] + lnb_ref[...]
    yb = _dot(((yn + bonus_ref[...]) * rg_ref[...]).astype(BF16), wb_ref[...])
    yc_ = _dot(mla_ref[...], wc_ref[...])
    yd = _dot(fn_ref[...].astype(BF16), wd_ref[...])
    g = gates_ref[...]
    d = D_MODEL
    mix = (g[:, 0:d].astype(F32) * ya + g[:, d:2 * d].astype(F32) * yb
           + g[:, 2 * d:3 * d].astype(F32) * yc_ + g[:, 3 * d:].astype(F32) * yd)
    o_ref[...] = x_ref[...] + _dot(mix.astype(BF16), wo_ref[...])


def _combine(x3, p3, yscan, bonus, rg, mla_o, fn, gates, cw, lng, lnb, avg,
             wa, wb, wc, wd, wo, *, ts):
    bsz, s_len, d = x3.shape
    main, prev, nxt = _halo_specs(ts, 3 * BW, P_CONV // (3 * BW), s_len)
    row = lambda w: pl.BlockSpec((None, ts, w), lambda b, i: (b, i, 0))
    full = lambda shape: pl.BlockSpec(shape, lambda b, i: (0,) * len(shape))
    yf = pl.BlockSpec((None, None, ts, BW), lambda b, i: (0, b, i, 0))
    yb = pl.BlockSpec((None, None, ts, BW), lambda b, i: (1, b, i, 0))
    return pl.pallas_call(
        _combine_kernel,
        grid=(bsz, s_len // ts),
        in_specs=[row(d), main, prev, nxt, yf, yb, row(BW), row(BW),
                  row(NH * 128), row(BW), row(N_BRANCH * d),
                  full((3, BW)), full((1, BW)), full((1, BW)), full((BW, BW)),
                  full((BW, d)), full((BW, d)), full((NH * 128, d)), full((BW, d)),
                  full((d, d))],
        out_specs=row(d),
        out_shape=jax.ShapeDtypeStruct(x3.shape, F32),
        compiler_params=_cparams(("parallel", "parallel")),
        name="combine",
    )(x3, p3, p3, p3, yscan, yscan, bonus, rg, mla_o, fn, gates, cw, lng, lnb, avg,
      wa, wb, wc, wd, wo)


def _ffn_kernel(x_ref, g_ref, wg_ref, wu_ref, wd_ref, fg_ref, o_ref, h_ref, acc_ref, *, final):
    j = pl.program_id(1)

    @pl.when(j == 0)
    def _():
        h_ref[...] = _rms(x_ref[...], g_ref[...]).astype(BF16)
        acc_ref[...] = jnp.zeros_like(acc_ref)

    h = h_ref[...]
    gt = _dot(h, wg_ref[...])
    up = _dot(h, wu_ref[...])
    act = (gt * _sigmoid(gt) * up).astype(BF16)
    acc_ref[...] += _dot(act, wd_ref[...])

    @pl.when(j == pl.num_programs(1) - 1)
    def _():
        y = x_ref[...] + acc_ref[...]
        if final:
            y = _rms(y, fg_ref[...])
        o_ref[...] = y


def _ffn(x, g, wgu, wd, fg, *, tm, tf, final):
    n, d = x.shape
    nf = D_FF // tf
    return pl.pallas_call(
        functools.partial(_ffn_kernel, final=final),
        grid=(n // tm, nf),
        in_specs=[pl.BlockSpec((tm, d), lambda i, j: (i, 0)),
                  pl.BlockSpec((1, d), lambda i, j: (0, 0)),
                  pl.BlockSpec((d, tf), lambda i, j: (0, j)),
                  pl.BlockSpec((d, tf), lambda i, j: (0, j + nf)),
                  pl.BlockSpec((tf, d), lambda i, j: (j, 0)),
                  pl.BlockSpec((1, d), lambda i, j: (0, 0))],
        out_specs=pl.BlockSpec((tm, d), lambda i, j: (i, 0)),
        out_shape=jax.ShapeDtypeStruct((n, d), F32),
        scratch_shapes=[pltpu.VMEM((tm, d), BF16), pltpu.VMEM((tm, d), F32)],
        compiler_params=_cparams(("parallel", "arbitrary")),
        name="ffn",
    )(x, g, wgu, wgu, wd, fg)


def _head_cols(w, widths, total=128):
    k = w.shape[0]
    per = sum(widths)
    w = w.reshape(k, NH, per)
    w = jnp.pad(w, ((0, 0), (0, 0), (0, total - per)))
    return w.reshape(k, NH * total)


def _rope_swap_cols(w):
    k = w.shape[0]
    w = w.reshape(k, NH, 128)
    half = QK_ROPE // 2
    x1 = w[:, :, QK_NOPE:QK_NOPE + half]
    x2 = w[:, :, QK_NOPE + half:QK_NOPE + QK_ROPE]
    z = jnp.zeros_like(w)
    z = z.at[:, :, QK_NOPE:QK_NOPE + half].set(-x2)
    z = z.at[:, :, QK_NOPE + half:QK_NOPE + QK_ROPE].set(x1)
    return z.reshape(k, NH * 128)


def _pick(total, pref):
    t = min(total, pref)
    while total % t:
        t //= 2
    return t


def kernel(x, positions, mix_norm, w_in, gate_bias, conv_w, conv_out, rwkv_mu, rwkv_w0, rwkv_w_up, rwkv_a0, rwkv_a_up, rwkv_g_up, rwkv_k_k, rwkv_k_a, rwkv_r_k, rwkv_ln_g, rwkv_ln_b, rwkv_out, mla_q_norm, mla_w_uq, mla_kv_norm, mla_w_ukv, mla_out, fnet_out, w_o, ffn_norm, ffn_w_gu, ffn_w_down, final_norm):
    bsz, s_len, d = x.shape
    n = bsz * s_len
    depth = w_in.shape[0]
    n1 = s_len // FFT_N2

    inv_freq = ROPE_THETA ** (-jnp.arange(0, QK_ROPE, 2, dtype=F32) / QK_ROPE)
    ang = positions.astype(F32)[..., None] * inv_freq
    cos, sin = jnp.cos(ang), jnp.sin(ang)
    ones = jnp.ones((bsz, s_len, QK_NOPE), F32)
    zpad = jnp.zeros((bsz, s_len, 128 - QK_NOPE - QK_ROPE), F32)
    cs_tab = jnp.concatenate([ones, cos, cos, zpad], axis=-1).reshape(n, 128)
    sn_tab = jnp.concatenate([0 * ones, sin, sin, zpad], axis=-1).reshape(n, 128)

    c1, s1, twc, tws, c2, s2, cg, sg = _fourier_constants(s_len)
    head_ones = jnp.asarray(np.kron(np.eye(NH), np.ones((HD, HD))), F32)
    head_avg = head_ones / HD

    place = np.zeros((128, NH * 128), np.float32)
    for h in range(NH):
        for jj in range(QK_ROPE):
            place[jj, h * 128 + QK_NOPE + jj] = 1.0
    place = jnp.asarray(place)
    place_sw = _rope_swap_cols(place)

    tm = _pick(n, 1024)
    ts = _pick(s_len, 512)
    tq = _pick(s_len, 1024)
    cuts = np.cumsum([768, 768, 384, 256, 160, 256])
    zero_bias = jnp.zeros((1, P_W), F32)

    xf = x.reshape(n, d)
    for l in range(depth):
        w = w_in[l]
        w_small = jnp.concatenate(
            [w[:, :cuts[2]], jnp.zeros((d, P_Q - P_LORA - 384), F32),
             w[:, cuts[2]:cuts[4]], jnp.zeros((d, 256 - 160), F32),
             w[:, cuts[4]:cuts[5]], jnp.zeros((d, P_W - P_F - 256), F32)], axis=1).astype(BF16)
        w_gate = w[:, cuts[5]:].astype(BF16)
        g_mix = mix_norm[l].reshape(1, d)
        p2 = _norm_mm(xf, g_mix, w_small, zero_bias, tm=_pick(n, 512), tn=P_W,
                      gate=False, out_dtype=F32)
        gates = _norm_mm(xf, g_mix, w_gate, gate_bias[l].reshape(1, N_BRANCH * d),
                         tm=_pick(n, 512), tn=N_BRANCH * d, gate=True, out_dtype=BF16)
        p3 = p2.reshape(bsz, s_len, P_W)

        r_, lw_, kt_, v_, kk_, b_, bonus, rg = _rwkv_prep(
            p3, rwkv_mu[l].reshape(2, 3 * BW), rwkv_w0[l], rwkv_w_up[l], rwkv_a0[l],
            rwkv_a_up[l], rwkv_g_up[l], rwkv_k_k[l].reshape(1, BW), rwkv_k_a[l].reshape(1, BW),
            rwkv_r_k[l].reshape(1, BW), head_ones, ts=ts)
        yscan = _rwkv_scan(r_, lw_, kt_, v_, kk_, b_, tb=_pick(s_len, 256))

        wq = _head_cols(mla_w_uq[l], (QK_NOPE, QK_ROPE))
        wkv = mla_w_ukv[l].reshape(KV_LORA, NH, 2 * HD)
        wk = _head_cols(wkv[:, :, :HD].reshape(KV_LORA, NH * HD), (HD,))
        wv = _head_cols(wkv[:, :, HD:].reshape(KV_LORA, NH * HD), (HD,))
        q, k, v = _mla_prep(p2, cs_tab, sn_tab, mla_q_norm[l].reshape(1, Q_LORA),
                            wq.astype(BF16), _rope_swap_cols(wq).astype(BF16),
                            mla_kv_norm[l].reshape(1, KV_LORA), wk.astype(BF16),
                            wv.astype(BF16), place, place_sw, ts=ts)
        hw = NH * 128
        mla_o = _attention(q.reshape(bsz, s_len, hw), k.reshape(bsz, s_len, hw),
                           v.reshape(bsz, s_len, hw), tq=tq, tk=tq)
        wc = jnp.pad(mla_out[l].reshape(NH, HD, d), ((0, 0), (0, 128 - HD), (0, 0))).reshape(hw, d)

        f_in = p3[:, :, P_F:P_F + BW].reshape(bsz, n1, FFT_N2 * BW)
        tr, ti = _fft1(f_in, c1, s1, twc, tws, wb=_pick(FFT_N2 * BW, 4096))
        fn = _fft2(tr.reshape(bsz, n1, FFT_N2, BW), ti.reshape(bsz, n1, FFT_N2, BW),
                   c2, s2, cg, sg, kb=_pick(n1, 4))
        fn = fn.reshape(bsz, s_len, BW)

        x3 = _combine(xf.reshape(bsz, s_len, d), p3, yscan, bonus, rg, mla_o, fn,
                      gates.reshape(bsz, s_len, N_BRANCH * d), conv_w[l],
                      rwkv_ln_g[l].reshape(1, BW), rwkv_ln_b[l].reshape(1, BW), head_avg,
                      conv_out[l].astype(BF16), rwkv_out[l].astype(BF16), wc.astype(BF16),
                      fnet_out[l].astype(BF16), w_o[l].astype(BF16), ts=ts)
        xf = _ffn(x3.reshape(n, d), ffn_norm[l].reshape(1, d), ffn_w_gu[l].astype(BF16),
                  ffn_w_down[l].astype(BF16), final_norm.reshape(1, d),
                  tm=tm, tf=256, final=(l == depth - 1))
    return xf.reshape(bsz, s_len, d)
```

```python
import functools

import numpy as np
import jax
import jax.numpy as jnp
from jax import lax
from jax.experimental import pallas as pl
from jax.experimental.pallas import tpu as pltpu

F32 = jnp.float32
BF16 = jnp.bfloat16
HI = lax.Precision.HIGHEST

D_MODEL = 1024
N_BRANCH = 4
BW = 256
HD = 64
NH = BW // HD
LORA = 64
GATE_LORA = 128
Q_LORA = 256
KV_LORA = 128
QK_NOPE = 64
QK_ROPE = 32
D_FF = 2816
NORM_EPS = 1e-6
RWKV_LN_EPS = 64e-5
ROPE_THETA = 10000.0
CHUNK = 64
FFT_N2 = 128

P_CONV = 0
P_RKV = 768
P_LORA = 1536
P_Q = 2048
P_KV = 2304
P_F = 2560
P_W = 2816

VMEM_LIMIT = 56 * 1024 * 1024


def _cparams(sem):
    return pltpu.CompilerParams(dimension_semantics=sem, vmem_limit_bytes=VMEM_LIMIT)


def _dot(a, b, prec=None):
    return jnp.dot(a, b, preferred_element_type=F32, precision=prec)


def _dot_nt(a, b, prec=None):
    return lax.dot_general(a, b, (((1,), (1,)), ((), ())),
                           preferred_element_type=F32, precision=prec)


def _dot_tn(a, b, prec=None):
    return lax.dot_general(a, b, (((0,), (0,)), ((), ())),
                           preferred_element_type=F32, precision=prec)


def _rms(xf, g, eps=NORM_EPS):
    return xf * lax.rsqrt(jnp.mean(xf * xf, axis=-1, keepdims=True) + eps) * g


def _sigmoid(z):
    return 1.0 / (1.0 + jnp.exp(-z))


def _norm_mm_kernel(x_ref, g_ref, w_ref, b_ref, o_ref, h_ref, *, gate):
    @pl.when(pl.program_id(1) == 0)
    def _():
        h_ref[...] = _rms(x_ref[...], g_ref[...]).astype(BF16)

    acc = _dot(h_ref[...], w_ref[...])
    if gate:
        acc = _sigmoid(acc + b_ref[...])
    o_ref[...] = acc.astype(o_ref.dtype)


def _norm_mm(x, g, w, b, *, tm, tn, gate, out_dtype):
    n, d = x.shape
    nc = w.shape[1]
    return pl.pallas_call(
        functools.partial(_norm_mm_kernel, gate=gate),
        grid=(n // tm, nc // tn),
        in_specs=[
            pl.BlockSpec((tm, d), lambda i, j: (i, 0)),
            pl.BlockSpec((1, d), lambda i, j: (0, 0)),
            pl.BlockSpec((d, tn), lambda i, j: (0, j)),
            pl.BlockSpec((1, tn), lambda i, j: (0, j)),
        ],
        out_specs=pl.BlockSpec((tm, tn), lambda i, j: (i, j)),
        out_shape=jax.ShapeDtypeStruct((n, nc), out_dtype),
        scratch_shapes=[pltpu.VMEM((tm, d), BF16)],
        compiler_params=_cparams(("parallel", "arbitrary")),
        name="norm_mm_gate" if gate else "norm_mm",
    )(x, g, w, b)


def _shifted(t, prev_blk, next_blk, i, n_i):
    ts = t.shape[0]
    prev_row = jnp.where(i == 0, 0.0, prev_blk[7:8, :])
    next_row = jnp.where(i == n_i - 1, 0.0, next_blk[0:1, :])
    rows = lax.broadcasted_iota(jnp.int32, (ts, 1), 0)
    t_prev = jnp.where(rows == 0, prev_row, pltpu.roll(t, 1, axis=0))
    t_next = jnp.where(rows == ts - 1, next_row, pltpu.roll(t, ts - 1, axis=0))
    return t_prev, t_next


def _halo_specs(ts, width, col_blk, s_len):
    r8 = ts // 8
    last8 = s_len // 8 - 1
    main = pl.BlockSpec((None, ts, width), lambda b, i: (b, i, col_blk))
    prev = pl.BlockSpec((None, 8, width),
                        lambda b, i: (b, jnp.maximum(i * r8 - 1, 0), col_blk))
    nxt = pl.BlockSpec((None, 8, width),
                       lambda b, i: (b, jnp.minimum((i + 1) * r8, last8), col_blk))
    return main, prev, nxt


def _softplus(z):
    return jnp.maximum(z, 0.0) + jnp.log(1.0 + jnp.exp(-jnp.abs(z)))


def _rwkv_prep_kernel(rkv_ref, prev_ref, next_ref, lora_ref, mu_ref, w0_ref, wup_ref,
                      a0_ref, aup_ref, gup_ref, kk_ref, ka_ref, rk_ref, bd_ref,
                      at_out, rt_out, bt_out, kt_out, bh_out, kh_out, v_out, gc_out,
                      bonus_out, g_out):
    i = pl.program_id(1)
    n_i = pl.num_programs(1)
    t = rkv_ref[...]
    ts = t.shape[0]
    nck = ts // CHUNK
    t_prev, t_next = _shifted(t, prev_ref[...], next_ref[...], i, n_i)
    lora = lora_ref[...]
    bd = bd_ref[...]
    bonus = None
    per_dir = []
    for d in range(2):
        sh = t_prev if d == 0 else t_next
        mixed = t + mu_ref[d:d + 1, :] * (sh - t)
        rd = mixed[:, 0:BW]
        kd = mixed[:, BW:2 * BW]
        vd = mixed[:, 2 * BW:3 * BW]
        w_l = jnp.tanh(lora[:, d * LORA:(d + 1) * LORA])
        a_l = lora[:, 2 * LORA + d * LORA:2 * LORA + (d + 1) * LORA]
        w_pre = w0_ref[d:d + 1, :] + _dot(w_l, wup_ref[d], HI)
        w_log = -_softplus(-w_pre) - 0.5
        lw = -jnp.exp(w_log)
        a = _sigmoid(a0_ref[d:d + 1, :] + _dot(a_l, aup_ref[d], HI))
        kk = kd * kk_ref[...]
        ss = _dot(kk * kk, bd, HI)
        kk = kk / jnp.maximum(jnp.sqrt(ss), 1e-12)
        kt = kd * (1.0 + (a - 1.0) * ka_ref[...])
        bo = _dot(rd * kt * rk_ref[...], bd, HI) * vd
        bonus = bo if bonus is None else bonus + bo
        per_dir.append((rd, kt, vd, kk, kk * a, lw))
    bonus_out[...] = bonus
    g_out[...] = _dot(_sigmoid(lora[:, 4 * LORA:4 * LORA + GATE_LORA]), gup_ref[...], HI)

    rows = lax.broadcasted_iota(jnp.int32, (ts, ts), 0)
    cols = lax.broadcasted_iota(jnp.int32, (ts, ts), 1)
    tri = jnp.where((rows // CHUNK == cols // CHUNK) & (cols <= rows), 1.0, 0.0).astype(F32)
    pre = _dot(tri, jnp.concatenate([per_dir[0][5], per_dir[1][5]], axis=1), HI)
    for d in range(2):
        rd, kt, vd, kk, b, lw = per_dir[d]
        tot_c = jnp.sum(lw.reshape(nck, CHUNK, BW), axis=1)
        tot = jnp.broadcast_to(tot_c[:, None, :], (nck, CHUNK, BW)).reshape(ts, BW)
        p = pre[:, d * BW:(d + 1) * BW]
        cum = p if d == 0 else tot - p + lw
        g_inv = jnp.exp(-cum)
        g_rem = jnp.exp(tot - cum)
        at_out[d] = (-kk * jnp.exp(cum - lw)).astype(BF16)
        rt_out[d] = (rd * jnp.exp(cum)).astype(BF16)
        bt_out[d] = (b * g_inv).astype(BF16)
        kt_out[d] = (kt * g_inv).astype(BF16)
        bh_out[d] = (b * g_rem).astype(BF16)
        kh_out[d] = (kt * g_rem).astype(BF16)
        v_out[d] = vd.astype(BF16)
        gc_out[d] = jnp.exp(tot_c)


def _rwkv_prep(p3, mu, w0, wup, a0, aup, gup, k_k, k_a, r_k, bd, *, ts):
    bsz, s_len, _ = p3.shape
    nck = ts // CHUNK
    main, prev, nxt = _halo_specs(ts, 3 * BW, P_RKV // (3 * BW), s_len)
    lora = pl.BlockSpec((None, ts, 384), lambda b, i: (b, i, P_LORA // 384))
    full = lambda shape: pl.BlockSpec(shape, lambda b, i: (0,) * len(shape))
    dir_out = pl.BlockSpec((2, None, ts, BW), lambda b, i: (0, b, i, 0))
    gc_out = pl.BlockSpec((2, None, nck, BW), lambda b, i: (0, b, i, 0))
    one_out = pl.BlockSpec((None, ts, BW), lambda b, i: (b, i, 0))
    dir_shape = jax.ShapeDtypeStruct((2, bsz, s_len, BW), BF16)
    gc_shape = jax.ShapeDtypeStruct((2, bsz, s_len // CHUNK, BW), F32)
    one_shape = jax.ShapeDtypeStruct((bsz, s_len, BW), F32)
    return pl.pallas_call(
        _rwkv_prep_kernel,
        grid=(bsz, s_len // ts),
        in_specs=[main, prev, nxt, lora,
                  full((2, 3 * BW)), full((2, BW)), full((2, LORA, BW)),
                  full((2, BW)), full((2, LORA, BW)), full((GATE_LORA, BW)),
                  full((1, BW)), full((1, BW)), full((1, BW)), full((BW, BW))],
        out_specs=[dir_out] * 7 + [gc_out] + [one_out] * 2,
        out_shape=[dir_shape] * 7 + [gc_shape] + [one_shape] * 2,
        compiler_params=_cparams(("parallel", "parallel")),
        name="rwkv_prep",
    )(p3, p3, p3, p3, mu, w0, wup, a0, aup, gup, k_k, k_a, r_k, bd)


def _rwkv_scan_kernel(*refs, n_chunk, bsz):
    n_in = 8
    fwd = refs[0:n_in]
    bwd = refs[n_in:2 * n_in]
    bdm_ref = refs[2 * n_in]
    yf_ref, yb_ref, h_ref = refs[2 * n_in + 1:]

    @pl.when(pl.program_id(0) == 0)
    def _():
        h_ref[...] = jnp.zeros_like(h_ref)

    bdm = bdm_ref[...]
    rows = lax.broadcasted_iota(jnp.int32, (CHUNK, BW), 0)
    cols = lax.broadcasted_iota(jnp.int32, (CHUNK, BW), 1) % CHUNK
    eye = jnp.where(rows == cols, 1.0, 0.0).astype(F32)

    def bd(x):
        xb = x.astype(BF16)
        return jnp.concatenate([xb] * NH, axis=0) * bdm

    def chunk_body(c, carry):
        seqs = [(d, b) for d in range(2) for b in range(bsz)]
        ns = len(seqs)
        ld = []
        for d, b in seqs:
            in_refs = fwd if d == 0 else bwd
            cidx = c if d == 0 else n_chunk - 1 - c
            sl = pl.ds(pl.multiple_of(cidx * CHUNK, CHUNK), CHUNK)
            ld.append([r[b, sl, :] for r in in_refs[:7]]
                      + [in_refs[7][b, pl.ds(cidx, 1), :], sl])
        strict = [(rows > cols) if d == 0 else (rows < cols) for d, _ in seqs]
        incl = [(rows >= cols) if d == 0 else (rows <= cols) for d, _ in seqs]
        m = [_dot_nt(jnp.concatenate([x[0], x[1]], axis=0),
                     jnp.concatenate([jnp.concatenate([x[2]] * NH, axis=0) * bdm,
                                      jnp.concatenate([x[3]] * NH, axis=0) * bdm], axis=0))
             for x in ld]
        a_ab = [jnp.where(strict[s], m[s][:CHUNK, :BW], 0.0) for s in range(ns)]
        a_ak = [jnp.where(strict[s], m[s][:CHUNK, BW:], 0.0) for s in range(ns)]
        a_rb = [jnp.where(incl[s], m[s][CHUNK:, :BW], 0.0).astype(BF16) for s in range(ns)]
        a_rk = [jnp.where(incl[s], m[s][CHUNK:, BW:], 0.0) for s in range(ns)]
        avs = [_dot(jnp.concatenate([a_ak[s], a_rk[s]], axis=0).astype(BF16), bd(ld[s][6]))
               for s in range(ns)]
        tm = [eye + a for a in a_ab]
        pw = [_dot(a.astype(BF16), bd(a)) for a in a_ab]
        for _ in range(4):
            res = [_dot(jnp.concatenate([tm[s], pw[s]], axis=0).astype(BF16), bd(pw[s]))
                   for s in range(ns)]
            tm = [tm[s] + res[s][:CHUNK] for s in range(ns)]
            pw = [r[CHUNK:] for r in res]
        tm = [(tm[s] + _dot(tm[s].astype(BF16), bd(pw[s]))).astype(BF16) for s in range(ns)]
        p = [_dot(tm[s], bd(ld[s][0])) for s in range(ns)]
        q = [_dot(tm[s], bd(avs[s][:CHUNK])) for s in range(ns)]
        g = [ld[s][1].astype(F32) + _dot(a_rb[s], bd(p[s])) for s in range(ns)]
        y0 = [avs[s][CHUNK:] + _dot(a_rb[s], bd(q[s])) for s in range(ns)]
        hst = [h_ref[s] for s in range(ns)]
        gp = [_dot_nt(jnp.concatenate([g[s], p[s]], axis=0).astype(BF16), hst[s].astype(BF16))
              for s in range(ns)]
        for s, (d, b) in enumerate(seqs):
            y_ref = yf_ref if d == 0 else yb_ref
            y_ref[b, ld[s][8], :] = gp[s][:CHUNK] + y0[s]
        upd = [_dot_tn(jnp.concatenate([(gp[s][CHUNK:] + q[s]).astype(BF16), ld[s][6]], axis=0),
                       jnp.concatenate([ld[s][4], ld[s][5]], axis=0)) for s in range(ns)]
        for s in range(ns):
            h_ref[s] = hst[s] * ld[s][7] + upd[s] * bdm.astype(F32)
        return carry

    lax.fori_loop(0, n_chunk, chunk_body, 0)


def _rwkv_scan(ops, gc, bdm, *, tb):
    _, bsz, s_len, _ = ops[0].shape
    nb = s_len // tb
    nck = tb // CHUNK

    def specs(d):
        blk = (lambda i: i) if d == 0 else (lambda i: nb - 1 - i)
        big = pl.BlockSpec((None, bsz, tb, BW), lambda i: (d, 0, blk(i), 0))
        small = pl.BlockSpec((None, bsz, nck, BW), lambda i: (d, 0, blk(i), 0))
        return [big] * 7 + [small]

    out_f = pl.BlockSpec((bsz, tb, BW), lambda i: (0, i, 0))
    out_b = pl.BlockSpec((bsz, tb, BW), lambda i: (0, nb - 1 - i, 0))
    shape = jax.ShapeDtypeStruct((bsz, s_len, BW), F32)
    args = list(ops) + [gc]
    return pl.pallas_call(
        functools.partial(_rwkv_scan_kernel, n_chunk=nck, bsz=bsz),
        grid=(nb,),
        in_specs=specs(0) + specs(1) + [pl.BlockSpec((BW, BW), lambda i: (0, 0))],
        out_specs=[out_f, out_b],
        out_shape=[shape, shape],
        scratch_shapes=[pltpu.VMEM((2 * bsz, BW, BW), F32)],
        compiler_params=_cparams(("arbitrary",)),
        name="rwkv_scan",
    )(*args, *args, bdm)


def _mla_prep_kernel(qlo_ref, kvlo_ref, cs_ref, sn_ref, qn_ref, wq_ref, wqs_ref,
                     kvn_ref, wk_ref, wv_ref, pl_ref, pls_ref, q_out, k_out, v_out, *, scale):
    cs = jnp.concatenate([cs_ref[...]] * NH, axis=1)
    sn = jnp.concatenate([sn_ref[...]] * NH, axis=1)
    hq = _rms(qlo_ref[...], qn_ref[...]).astype(BF16)
    q = _dot(hq, wq_ref[...]) * cs + _dot(hq, wqs_ref[...]) * sn
    q_out[...] = (q * scale).astype(BF16)
    kv = kvlo_ref[...]
    hkv = _rms(kv[:, :KV_LORA], kvn_ref[...]).astype(BF16)
    kr = kv[:, KV_LORA:]
    k = (_dot(hkv, wk_ref[...]) + _dot(kr, pl_ref[...], HI)) * cs + _dot(kr, pls_ref[...], HI) * sn
    k_out[...] = k.astype(BF16)
    v = _dot(hkv, wv_ref[...])
    lane = lax.broadcasted_iota(jnp.int32, v.shape, 1) % 128
    v_out[...] = jnp.where(lane == HD, 1.0, v).astype(BF16)


def _mla_prep(p2, cs, sn, qn, wq, wqs, kvn, wk, wv, plc, pls, *, ts):
    n = p2.shape[0]
    full = lambda shape: pl.BlockSpec(shape, lambda i: (0,) * len(shape))
    hw = NH * 128
    out = pl.BlockSpec((ts, hw), lambda i: (i, 0))
    shape = jax.ShapeDtypeStruct((n, hw), BF16)
    return pl.pallas_call(
        functools.partial(_mla_prep_kernel, scale=float((QK_NOPE + QK_ROPE) ** -0.5 * np.log2(np.e))),
        grid=(n // ts,),
        in_specs=[pl.BlockSpec((ts, 256), lambda i: (i, P_Q // 256)),
                  pl.BlockSpec((ts, 256), lambda i: (i, P_KV // 256)),
                  pl.BlockSpec((ts, 128), lambda i: (i, 0)),
                  pl.BlockSpec((ts, 128), lambda i: (i, 0)),
                  full((1, Q_LORA)), full((Q_LORA, hw)), full((Q_LORA, hw)),
                  full((1, KV_LORA)), full((KV_LORA, hw)), full((KV_LORA, hw)),
                  full((128, hw)), full((128, hw))],
        out_specs=[out] * 3,
        out_shape=[shape] * 3,
        compiler_params=_cparams(("parallel",)),
        name="mla_prep",
    )(p2, p2, cs, sn, qn, wq, wqs, kvn, wk, wv, plc, pls)


def _attn_kernel(q_ref, k_ref, v_ref, o_ref, m_ref, acc_ref, *, rb):
    j = pl.program_id(2)

    @pl.when(j == 0)
    def _():
        m_ref[...] = jnp.full_like(m_ref, -jnp.inf)
        acc_ref[...] = jnp.zeros_like(acc_ref)

    units = [(h, r) for h in range(NH) for r in range(q_ref.shape[0] // rb)]

    def scores(h, r):
        hs = slice(h * 128, (h + 1) * 128)
        return _dot_nt(q_ref[r * rb:(r + 1) * rb, hs], k_ref[:, hs])

    s_next = scores(*units[0])
    for idx, (h, r) in enumerate(units):
        s = s_next
        if idx + 1 < len(units):
            s_next = scores(*units[idx + 1])
        rows = slice(r * rb, (r + 1) * rb)
        m_old = m_ref[h, rows]
        m_new = jnp.maximum(m_old, jnp.max(s, axis=-1, keepdims=True))
        p = jnp.exp2(s - m_new).astype(BF16)
        alpha = jnp.exp2(m_old - m_new)
        acc_ref[h, rows] = alpha * acc_ref[h, rows] + _dot(p, v_ref[:, h * 128:(h + 1) * 128])
        m_ref[h, rows] = m_new

    @pl.when(j == pl.num_programs(2) - 1)
    def _():
        for h in range(NH):
            acc = acc_ref[h]
            o_ref[:, h * 128:(h + 1) * 128] = (acc / acc[:, HD:HD + 1]).astype(o_ref.dtype)


def _attention(q, k, v, *, tq, tk):
    bsz, s_len, hw = q.shape
    return pl.pallas_call(
        functools.partial(_attn_kernel, rb=_pick(tq, 256)),
        grid=(bsz, s_len // tq, s_len // tk),
        in_specs=[pl.BlockSpec((None, tq, hw), lambda b, i, j: (b, i, 0)),
                  pl.BlockSpec((None, tk, hw), lambda b, i, j: (b, j, 0)),
                  pl.BlockSpec((None, tk, hw), lambda b, i, j: (b, j, 0))],
        out_specs=pl.BlockSpec((None, tq, hw), lambda b, i, j: (b, i, 0)),
        out_shape=jax.ShapeDtypeStruct((bsz, s_len, hw), BF16),
        scratch_shapes=[pltpu.VMEM((NH, tq, 1), F32), pltpu.VMEM((NH, tq, 128), F32)],
        compiler_params=_cparams(("parallel", "parallel", "arbitrary")),
        name="mla_attention",
    )(q, k, v)


def _fft1_kernel(x_ref, c1_ref, s1_ref, tc_ref, ts_ref, tr_out, ti_out):
    x = x_ref[...]
    ar = _dot(c1_ref[...], x, HI)
    ai = -_dot(s1_ref[...], x, HI)
    tc = tc_ref[...]
    tsn = ts_ref[...]
    tr_out[...] = ar * tc + ai * tsn
    ti_out[...] = ai * tc - ar * tsn


def _fft1(xf, c1, s1, twc, tws, *, wb):
    bsz, n1, wide = xf.shape
    blk = pl.BlockSpec((None, n1, wb), lambda b, j: (b, 0, j))
    tw = pl.BlockSpec((n1, wb), lambda b, j: (0, j))
    mat = pl.BlockSpec((n1, n1), lambda b, j: (0, 0))
    shape = jax.ShapeDtypeStruct((bsz, n1, wide), F32)
    return pl.pallas_call(
        _fft1_kernel,
        grid=(bsz, wide // wb),
        in_specs=[blk, mat, mat, tw, tw],
        out_specs=[blk, blk],
        out_shape=[shape, shape],
        compiler_params=_cparams(("parallel", "parallel")),
        name="fft_stage1",
    )(xf, c1, s1, twc, tws)


def _fft2_kernel(tr_ref, ti_ref, c2_ref, s2_ref, cg_ref, sg_ref, o_ref, *, kb):
    c2 = c2_ref[...]
    s2 = s2_ref[...]
    for q in range(kb):
        tr = tr_ref[q]
        ti = ti_ref[q]
        ur = _dot(c2, tr, HI) + _dot(s2, ti, HI)
        ui = _dot(c2, ti, HI) - _dot(s2, tr, HI)
        o_ref[:, q * BW:(q + 1) * BW] = _dot(ur, cg_ref[...], HI) + _dot(ui, sg_ref[...], HI)


def _fft2(tr, ti, c2, s2, cg, sg, *, kb):
    bsz, n1, n2, _ = tr.shape
    blk = pl.BlockSpec((None, kb, n2, BW), lambda b, j: (b, j, 0, 0))
    full = lambda shape: pl.BlockSpec(shape, lambda b, j: (0,) * len(shape))
    return pl.pallas_call(
        functools.partial(_fft2_kernel, kb=kb),
        grid=(bsz, n1 // kb),
        in_specs=[blk, blk, full((n2, n2)), full((n2, n2)), full((BW, BW)), full((BW, BW))],
        out_specs=pl.BlockSpec((None, n2, kb * BW), lambda b, j: (b, 0, j)),
        out_shape=jax.ShapeDtypeStruct((bsz, n2, n1 * BW), F32),
        compiler_params=_cparams(("parallel", "parallel")),
        name="fft_stage2",
    )(tr, ti, c2, s2, cg, sg)


def _dft_mats(n):
    idx = np.arange(n)
    ang = 2.0 * np.pi * ((idx[:, None] * idx[None, :]) % n) / n
    return np.cos(ang), np.sin(ang)


def _fourier_constants(s_len):
    n2 = FFT_N2
    n1 = s_len // n2
    c1, s1 = _dft_mats(n1)
    c2, s2 = _dft_mats(n2)
    k1 = np.arange(n1)[:, None]
    m2 = np.arange(n2)[None, :]
    ang = 2.0 * np.pi * ((k1 * m2) % s_len) / s_len
    twc, tws = np.cos(ang), np.sin(ang)
    cg64, sg64 = _dft_mats(HD)
    norm = 1.0 / np.sqrt(float(s_len) * HD)
    cg = np.kron(np.eye(NH), cg64) * norm
    sg = np.kron(np.eye(NH), sg64) * norm
    f = lambda a: jnp.asarray(a, F32)
    twc = jnp.repeat(f(twc), BW, axis=1)
    tws = jnp.repeat(f(tws), BW, axis=1)
    return f(c1), f(s1), twc, tws, f(c2), f(s2), f(cg), f(sg)


def _combine_kernel(x_ref, cv_ref, cvp_ref, cvn_ref, yf_ref, yb_ref, bonus_ref, rg_ref,
                    mla_ref, fn_ref, gates_ref, cw_ref, lng_ref, lnb_ref, avg_ref,
                    wa_ref, wb_ref, wc_ref, wd_ref, wo_ref, o_ref):
    i = pl.program_id(1)
    n_i = pl.num_programs(1)
    cv = cv_ref[...]
    cvp = cvp_ref[...]
    cvn = cvn_ref[...]
    z = cv[:, 2 * BW:] * cv[:, :BW]
    zp, zn = _shifted(z, cvp[:, 2 * BW:] * cvp[:, :BW], cvn[:, 2 * BW:] * cvn[:, :BW], i, n_i)
    conv = zp * cw_ref[0:1, :] + z * cw_ref[1:2, :] + zn * cw_ref[2:3, :]
    ya = _dot((cv[:, BW:2 * BW] * conv).astype(BF16), wa_ref[...])
    y = yf_ref[...] + yb_ref[...]
    avg = avg_ref[...]
    mean = _dot(y, avg, HI)
    yc = y - mean
    var = _dot(yc * yc, avg, HI)
    yn = yc * lax.rsqrt(var + RWKV_LN_EPS) * lng_ref[...] + lnb_ref[...]
    yb = _dot(((yn + bonus_ref[...]) * rg_ref[...]).astype(BF16), wb_ref[...])
    yc_ = _dot(mla_ref[...], wc_ref[...])
    yd = _dot(fn_ref[...].astype(BF16), wd_ref[...])
    g = gates_ref[...]
    d = D_MODEL
    mix = (g[:, 0:d].astype(F32) * ya + g[:, d:2 * d].astype(F32) * yb
           + g[:, 2 * d:3 * d].astype(F32) * yc_ + g[:, 3 * d:].astype(F32) * yd)
    o_ref[...] = x_ref[...] + _dot(mix.astype(BF16), wo_ref[...])


def _combine(x3, p3, yf, yb, bonus, rg, mla_o, fn, gates, cw, lng, lnb, avg,
             wa, wb, wc, wd, wo, *, ts):
    bsz, s_len, d = x3.shape
    main, prev, nxt = _halo_specs(ts, 3 * BW, P_CONV // (3 * BW), s_len)
    row = lambda w: pl.BlockSpec((None, ts, w), lambda b, i: (b, i, 0))
    full = lambda shape: pl.BlockSpec(shape, lambda b, i: (0,) * len(shape))
    return pl.pallas_call(
        _combine_kernel,
        grid=(bsz, s_len // ts),
        in_specs=[row(d), main, prev, nxt, row(BW), row(BW), row(BW), row(BW),
                  row(NH * 128), row(BW), row(N_BRANCH * d),
                  full((3, BW)), full((1, BW)), full((1, BW)), full((BW, BW)),
                  full((BW, d)), full((BW, d)), full((NH * 128, d)), full((BW, d)),
                  full((d, d))],
        out_specs=row(d),
        out_shape=jax.ShapeDtypeStruct(x3.shape, F32),
        compiler_params=_cparams(("parallel", "parallel")),
        name="combine",
    )(x3, p3, p3, p3, yf, yb, bonus, rg, mla_o, fn, gates, cw, lng, lnb, avg,
      wa, wb, wc, wd, wo)


def _ffn_kernel(x_ref, g_ref, wg_ref, wu_ref, wd_ref, fg_ref, o_ref, h_ref, acc_ref, *, final):
    j = pl.program_id(1)

    @pl.when(j == 0)
    def _():
        h_ref[...] = _rms(x_ref[...], g_ref[...]).astype(BF16)
        acc_ref[...] = jnp.zeros_like(acc_ref)

    h = h_ref[...]
    gt = _dot(h, wg_ref[...])
    up = _dot(h, wu_ref[...])
    act = (gt * _sigmoid(gt) * up).astype(BF16)
    acc_ref[...] += _dot(act, wd_ref[...])

    @pl.when(j == pl.num_programs(1) - 1)
    def _():
        y = x_ref[...] + acc_ref[...]
        if final:
            y = _rms(y, fg_ref[...])
        o_ref[...] = y


def _ffn(x, g, wgu, wd, fg, *, tm, tf, final):
    n, d = x.shape
    nf = D_FF // tf
    return pl.pallas_call(
        functools.partial(_ffn_kernel, final=final),
        grid=(n // tm, nf),
        in_specs=[pl.BlockSpec((tm, d), lambda i, j: (i, 0)),
                  pl.BlockSpec((1, d), lambda i, j: (0, 0)),
                  pl.BlockSpec((d, tf), lambda i, j: (0, j)),
                  pl.BlockSpec((d, tf), lambda i, j: (0, j + nf)),
                  pl.BlockSpec((tf, d), lambda i, j: (j, 0)),
                  pl.BlockSpec((1, d), lambda i, j: (0, 0))],
        out_specs=pl.BlockSpec((tm, d), lambda i, j: (i, 0)),
        out_shape=jax.ShapeDtypeStruct((n, d), F32),
        scratch_shapes=[pltpu.VMEM((tm, d), BF16), pltpu.VMEM((tm, d), F32)],
        compiler_params=_cparams(("parallel", "arbitrary")),
        name="ffn",
    )(x, g, wgu, wgu, wd, fg)


def _head_cols(w, widths, total=128):
    k = w.shape[0]
    per = sum(widths)
    w = w.reshape(k, NH, per)
    w = jnp.pad(w, ((0, 0), (0, 0), (0, total - per)))
    return w.reshape(k, NH * total)


def _rope_swap_cols(w):
    k = w.shape[0]
    w = w.reshape(k, NH, 128)
    half = QK_ROPE // 2
    x1 = w[:, :, QK_NOPE:QK_NOPE + half]
    x2 = w[:, :, QK_NOPE + half:QK_NOPE + QK_ROPE]
    z = jnp.zeros_like(w)
    z = z.at[:, :, QK_NOPE:QK_NOPE + half].set(-x2)
    z = z.at[:, :, QK_NOPE + half:QK_NOPE + QK_ROPE].set(x1)
    return z.reshape(k, NH * 128)


def _pick(total, pref):
    t = min(total, pref)
    while total % t:
        t //= 2
    return t


def kernel(x, positions, mix_norm, w_in, gate_bias, conv_w, conv_out, rwkv_mu, rwkv_w0, rwkv_w_up, rwkv_a0, rwkv_a_up, rwkv_g_up, rwkv_k_k, rwkv_k_a, rwkv_r_k, rwkv_ln_g, rwkv_ln_b, rwkv_out, mla_q_norm, mla_w_uq, mla_kv_norm, mla_w_ukv, mla_out, fnet_out, w_o, ffn_norm, ffn_w_gu, ffn_w_down, final_norm):
    bsz, s_len, d = x.shape
    n = bsz * s_len
    depth = w_in.shape[0]
    n1 = s_len // FFT_N2

    inv_freq = ROPE_THETA ** (-jnp.arange(0, QK_ROPE, 2, dtype=F32) / QK_ROPE)
    ang = positions.astype(F32)[..., None] * inv_freq
    cos, sin = jnp.cos(ang), jnp.sin(ang)
    ones = jnp.ones((bsz, s_len, QK_NOPE), F32)
    zpad = jnp.zeros((bsz, s_len, 128 - QK_NOPE - QK_ROPE), F32)
    cs_tab = jnp.concatenate([ones, cos, cos, zpad], axis=-1).reshape(n, 128)
    sn_tab = jnp.concatenate([0 * ones, sin, sin, zpad], axis=-1).reshape(n, 128)

    c1, s1, twc, tws, c2, s2, cg, sg = _fourier_constants(s_len)
    head_ones_np = np.kron(np.eye(NH), np.ones((HD, HD)))
    head_ones = jnp.asarray(head_ones_np, F32)
    head_avg = head_ones / HD
    head_mask = jnp.asarray(head_ones_np, BF16)

    place = np.zeros((128, NH * 128), np.float32)
    for h in range(NH):
        for jj in range(QK_ROPE):
            place[jj, h * 128 + QK_NOPE + jj] = 1.0
    place = jnp.asarray(place)
    place_sw = _rope_swap_cols(place)

    tm = _pick(n, 1024)
    ts = _pick(s_len, 512)
    tq = _pick(s_len, 1024)
    cuts = np.cumsum([768, 768, 384, 256, 160, 256])
    zero_bias = jnp.zeros((1, P_W), F32)

    xf = x.reshape(n, d)
    for l in range(depth):
        w = w_in[l]
        w_small = jnp.concatenate(
            [w[:, :cuts[2]], jnp.zeros((d, P_Q - P_LORA - 384), F32),
             w[:, cuts[2]:cuts[4]], jnp.zeros((d, 256 - 160), F32),
             w[:, cuts[4]:cuts[5]], jnp.zeros((d, P_W - P_F - 256), F32)], axis=1).astype(BF16)
        w_gate = w[:, cuts[5]:].astype(BF16)
        g_mix = mix_norm[l].reshape(1, d)
        p2 = _norm_mm(xf, g_mix, w_small, zero_bias, tm=_pick(n, 512), tn=P_W,
                      gate=False, out_dtype=F32)
        gates = _norm_mm(xf, g_mix, w_gate, gate_bias[l].reshape(1, N_BRANCH * d),
                         tm=_pick(n, 512), tn=N_BRANCH * d, gate=True, out_dtype=BF16)
        p3 = p2.reshape(bsz, s_len, P_W)

        *scan_ops, gc, bonus, rg = _rwkv_prep(
            p3, rwkv_mu[l].reshape(2, 3 * BW), rwkv_w0[l], rwkv_w_up[l], rwkv_a0[l],
            rwkv_a_up[l], rwkv_g_up[l], rwkv_k_k[l].reshape(1, BW), rwkv_k_a[l].reshape(1, BW),
            rwkv_r_k[l].reshape(1, BW), head_ones, ts=ts)
        yf, yb = _rwkv_scan(scan_ops, gc, head_mask, tb=ts)

        wq = _head_cols(mla_w_uq[l], (QK_NOPE, QK_ROPE))
        wkv = mla_w_ukv[l].reshape(KV_LORA, NH, 2 * HD)
        wk = _head_cols(wkv[:, :, :HD].reshape(KV_LORA, NH * HD), (HD,))
        wv = _head_cols(wkv[:, :, HD:].reshape(KV_LORA, NH * HD), (HD,))
        q, k, v = _mla_prep(p2, cs_tab, sn_tab, mla_q_norm[l].reshape(1, Q_LORA),
                            wq.astype(BF16), _rope_swap_cols(wq).astype(BF16),
                            mla_kv_norm[l].reshape(1, KV_LORA), wk.astype(BF16),
                            wv.astype(BF16), place, place_sw, ts=ts)
        hw = NH * 128
        mla_o = _attention(q.reshape(bsz, s_len, hw), k.reshape(bsz, s_len, hw),
                           v.reshape(bsz, s_len, hw), tq=tq, tk=tq)
        wc = jnp.pad(mla_out[l].reshape(NH, HD, d), ((0, 0), (0, 128 - HD), (0, 0))).reshape(hw, d)

        f_in = p3[:, :, P_F:P_F + BW].reshape(bsz, n1, FFT_N2 * BW)
        tr, ti = _fft1(f_in, c1, s1, twc, tws, wb=_pick(FFT_N2 * BW, 4096))
        fn = _fft2(tr.reshape(bsz, n1, FFT_N2, BW), ti.reshape(bsz, n1, FFT_N2, BW),
                   c2, s2, cg, sg, kb=_pick(n1, 4))
        fn = fn.reshape(bsz, s_len, BW)

        x3 = _combine(xf.reshape(bsz, s_len, d), p3, yf, yb, bonus, rg, mla_o, fn,
                      gates.reshape(bsz, s_len, N_BRANCH * d), conv_w[l],
                      rwkv_ln_g[l].reshape(1, BW), rwkv_ln_b[l].reshape(1, BW), head_avg,
                      conv_out[l].astype(BF16), rwkv_out[l].astype(BF16), wc.astype(BF16),
                      fnet_out[l].astype(BF16), w_o[l].astype(BF16), ts=ts)
        xf = _ffn(x3.reshape(n, d), ffn_norm[l].reshape(1, d), ffn_w_gu[l].astype(BF16),
                  ffn_w_down[l].astype(BF16), final_norm.reshape(1, d),
                  tm=tm, tf=256, final=(l == depth - 1))
    return xf.reshape(bsz, s_len, d)
```

```python
import functools

import numpy as np
import jax
import jax.numpy as jnp
from jax import lax
from jax.experimental import pallas as pl
from jax.experimental.pallas import tpu as pltpu

F32 = jnp.float32
BF16 = jnp.bfloat16

D_MODEL = 1024
N_BRANCH = 4
BW = 256
HD = 64
NH = BW // HD
LORA = 64
GATE_LORA = 128
Q_LORA = 256
KV_LORA = 128
QK_NOPE = 64
QK_ROPE = 32
D_FF = 2816
NORM_EPS = 1e-6
RWKV_LN_EPS = 64e-5
ROPE_THETA = 10000.0
CHUNK = 64
FFT_N2 = 128

P_CONV = 0
P_RKV = 768
P_LORA = 1536
P_Q = 2048
P_KV = 2304
P_F = 2560
P_W = 2816

VMEM_LIMIT = 56 * 1024 * 1024


def _cparams(sem):
    return pltpu.CompilerParams(dimension_semantics=sem, vmem_limit_bytes=VMEM_LIMIT)


def _resident(shape):
    return pl.BlockSpec(shape, lambda *_: (0,) * len(shape), pipeline_mode=pl.Buffered(1))


def _dot(a, b, prec=None):
    return jnp.dot(a, b, preferred_element_type=F32, precision=prec)


def _dot_nt(a, b, prec=None):
    return lax.dot_general(a, b, (((1,), (1,)), ((), ())),
                           preferred_element_type=F32, precision=prec)


def _dot_tn(a, b, prec=None):
    return lax.dot_general(a, b, (((0,), (0,)), ((), ())),
                           preferred_element_type=F32, precision=prec)


def _dotb(a, b):
    return _dot(a.astype(BF16), b.astype(BF16))


def _bf16_terms(x, terms):
    parts = []
    for _ in range(terms):
        p = x.astype(BF16)
        parts.append(p)
        x = x - p.astype(F32)
    return parts


def _dot_exact_rhs(x, m01, terms):
    return sum(_dot(p, m01) for p in _bf16_terms(x, terms))


def _rms(xf, g, eps=NORM_EPS):
    return xf * lax.rsqrt(jnp.mean(xf * xf, axis=-1, keepdims=True) + eps) * g


def _sigmoid(z):
    return 1.0 / (1.0 + jnp.exp(-z))


def _norm_mm_kernel(x_ref, g_ref, w_ref, o_ref):
    h = _rms(x_ref[...], g_ref[...]).astype(BF16)
    o_ref[...] = _dot(h, w_ref[...])


def _norm_mm(x, g, w, *, tm):
    n, d = x.shape
    nc = w.shape[1]
    return pl.pallas_call(
        _norm_mm_kernel,
        grid=(n // tm,),
        in_specs=[pl.BlockSpec((tm, d), lambda i: (i, 0)), _resident((1, d)),
                  _resident((d, nc))],
        out_specs=pl.BlockSpec((tm, nc), lambda i: (i, 0)),
        out_shape=jax.ShapeDtypeStruct((n, nc), F32),
        compiler_params=_cparams(("parallel",)),
        name="norm_mm",
    )(x, g, w)


def _shifted(t, prev_blk, next_blk, i, n_i):
    ts = t.shape[0]
    prev_row = jnp.where(i == 0, 0.0, prev_blk[7:8, :])
    next_row = jnp.where(i == n_i - 1, 0.0, next_blk[0:1, :])
    rows = lax.broadcasted_iota(jnp.int32, (ts, 1), 0)
    t_prev = jnp.where(rows == 0, prev_row, pltpu.roll(t, 1, axis=0))
    t_next = jnp.where(rows == ts - 1, next_row, pltpu.roll(t, ts - 1, axis=0))
    return t_prev, t_next


def _halo_specs(ts, width, col_blk, s_len):
    r8 = ts // 8
    last8 = s_len // 8 - 1
    main = pl.BlockSpec((None, ts, width), lambda b, i: (b, i, col_blk))
    prev = pl.BlockSpec((None, 8, width),
                        lambda b, i: (b, jnp.maximum(i * r8 - 1, 0), col_blk))
    nxt = pl.BlockSpec((None, 8, width),
                       lambda b, i: (b, jnp.minimum((i + 1) * r8, last8), col_blk))
    return main, prev, nxt


def _softplus(z):
    return jnp.maximum(z, 0.0) + jnp.log(1.0 + jnp.exp(-jnp.abs(z)))


def _rwkv_prep_kernel(rkv_ref, prev_ref, next_ref, lora_ref, mu_ref, w0_ref, wup_ref,
                      a0_ref, aup_ref, gup_ref, kk_ref, ka_ref, rk_ref, bd_ref,
                      at_out, rt_out, bt_out, kt_out, bh_out, kh_out, v_out, gc_out,
                      bonus_out, g_out):
    i = pl.program_id(1)
    n_i = pl.num_programs(1)
    t = rkv_ref[...]
    ts = t.shape[0]
    nck = ts // CHUNK
    t_prev, t_next = _shifted(t, prev_ref[...], next_ref[...], i, n_i)
    lora = lora_ref[...]
    bd = bd_ref[...]
    bonus = None
    per_dir = []
    for d in range(2):
        sh = t_prev if d == 0 else t_next
        mixed = t + mu_ref[d:d + 1, :] * (sh - t)
        rd = mixed[:, 0:BW]
        kd = mixed[:, BW:2 * BW]
        vd = mixed[:, 2 * BW:3 * BW]
        w_l = jnp.tanh(lora[:, d * LORA:(d + 1) * LORA])
        a_l = lora[:, 2 * LORA + d * LORA:2 * LORA + (d + 1) * LORA]
        w_pre = w0_ref[d:d + 1, :] + _dotb(w_l, wup_ref[d])
        w_log = -_softplus(-w_pre) - 0.5
        lw = -jnp.exp(w_log)
        a = _sigmoid(a0_ref[d:d + 1, :] + _dotb(a_l, aup_ref[d]))
        kk = kd * kk_ref[...]
        ss = _dot_exact_rhs(kk * kk, bd, 2)
        kk = kk / jnp.maximum(jnp.sqrt(ss), 1e-12)
        kt = kd * (1.0 + (a - 1.0) * ka_ref[...])
        bo = _dot_exact_rhs(rd * kt * rk_ref[...], bd, 2) * vd
        bonus = bo if bonus is None else bonus + bo
        per_dir.append((rd, kt, vd, kk, kk * a, lw))
    bonus_out[...] = bonus
    g_out[...] = _dotb(_sigmoid(lora[:, 4 * LORA:4 * LORA + GATE_LORA]), gup_ref[...])

    rows = lax.broadcasted_iota(jnp.int32, (ts, ts), 0)
    cols = lax.broadcasted_iota(jnp.int32, (ts, ts), 1)
    tri = jnp.where((rows // CHUNK == cols // CHUNK) & (cols <= rows), 1.0, 0.0).astype(BF16)
    lw_terms = _bf16_terms(jnp.concatenate([per_dir[0][5], per_dir[1][5]], axis=1), 3)
    pre = sum(_dot(tri, p) for p in lw_terms)
    for d in range(2):
        rd, kt, vd, kk, b, lw = per_dir[d]
        tot_c = jnp.sum(lw.reshape(nck, CHUNK, BW), axis=1)
        tot = jnp.broadcast_to(tot_c[:, None, :], (nck, CHUNK, BW)).reshape(ts, BW)
        p = pre[:, d * BW:(d + 1) * BW]
        cum = p if d == 0 else tot - p + lw
        g_inv = jnp.exp(-cum)
        g_rem = jnp.exp(tot - cum)
        at_out[d] = (-kk * jnp.exp(cum - lw)).astype(BF16)
        rt_out[d] = (rd * jnp.exp(cum)).astype(BF16)
        bt_out[d] = (b * g_inv).astype(BF16)
        kt_out[d] = (kt * g_inv).astype(BF16)
        bh_out[d] = (b * g_rem).astype(BF16)
        kh_out[d] = (kt * g_rem).astype(BF16)
        v_out[d] = vd.astype(BF16)
        gc_out[d] = jnp.exp(tot_c)


def _rwkv_prep(p3, mu, w0, wup, a0, aup, gup, k_k, k_a, r_k, bd, *, ts):
    bsz, s_len, _ = p3.shape
    nck = ts // CHUNK
    main, prev, nxt = _halo_specs(ts, 3 * BW, P_RKV // (3 * BW), s_len)
    lora = pl.BlockSpec((None, ts, 384), lambda b, i: (b, i, P_LORA // 384))
    full = lambda shape: pl.BlockSpec(shape, lambda b, i: (0,) * len(shape))
    dir_out = pl.BlockSpec((2, None, ts, BW), lambda b, i: (0, b, i, 0))
    gc_out = pl.BlockSpec((2, None, nck, BW), lambda b, i: (0, b, i, 0))
    one_out = pl.BlockSpec((None, ts, BW), lambda b, i: (b, i, 0))
    dir_shape = jax.ShapeDtypeStruct((2, bsz, s_len, BW), BF16)
    gc_shape = jax.ShapeDtypeStruct((2, bsz, s_len // CHUNK, BW), F32)
    one_shape = jax.ShapeDtypeStruct((bsz, s_len, BW), F32)
    return pl.pallas_call(
        _rwkv_prep_kernel,
        grid=(bsz, s_len // ts),
        in_specs=[main, prev, nxt, lora,
                  full((2, 3 * BW)), full((2, BW)), full((2, LORA, BW)),
                  full((2, BW)), full((2, LORA, BW)), full((GATE_LORA, BW)),
                  full((1, BW)), full((1, BW)), full((1, BW)), full((BW, BW))],
        out_specs=[dir_out] * 7 + [gc_out] + [one_out] * 2,
        out_shape=[dir_shape] * 7 + [gc_shape] + [one_shape] * 2,
        compiler_params=_cparams(("parallel", "parallel")),
        name="rwkv_prep",
    )(p3, p3, p3, p3, mu, w0, wup, a0, aup, gup, k_k, k_a, r_k, bd)


def _rwkv_scan_kernel(*refs, n_chunk, bsz):
    n_in = 8
    fwd = refs[0:n_in]
    bwd = refs[n_in:2 * n_in]
    bdm_ref = refs[2 * n_in]
    yf_ref, yb_ref, h_ref = refs[2 * n_in + 1:]

    @pl.when(pl.program_id(0) == 0)
    def _():
        h_ref[...] = jnp.zeros_like(h_ref)

    bdm = bdm_ref[...]
    rows = lax.broadcasted_iota(jnp.int32, (CHUNK, BW), 0)
    cols = lax.broadcasted_iota(jnp.int32, (CHUNK, BW), 1) % CHUNK
    eye = jnp.where(rows == cols, 1.0, 0.0).astype(F32)

    def bd(x):
        xb = x.astype(BF16)
        return jnp.concatenate([xb] * NH, axis=0) * bdm

    def chunk_body(c, carry):
        seqs = [(d, b) for d in range(2) for b in range(bsz)]
        ns = len(seqs)
        ld = []
        for d, b in seqs:
            in_refs = fwd if d == 0 else bwd
            cidx = c if d == 0 else n_chunk - 1 - c
            sl = pl.ds(pl.multiple_of(cidx * CHUNK, CHUNK), CHUNK)
            ld.append([r[b, sl, :] for r in in_refs[:7]]
                      + [in_refs[7][b, pl.ds(cidx, 1), :], sl])
        strict = [(rows > cols) if d == 0 else (rows < cols) for d, _ in seqs]
        incl = [(rows >= cols) if d == 0 else (rows <= cols) for d, _ in seqs]
        m = [_dot_nt(jnp.concatenate([x[0], x[1]], axis=0),
                     jnp.concatenate([jnp.concatenate([x[2]] * NH, axis=0) * bdm,
                                      jnp.concatenate([x[3]] * NH, axis=0) * bdm], axis=0))
             for x in ld]
        a_ab = [jnp.where(strict[s], m[s][:CHUNK, :BW], 0.0) for s in range(ns)]
        a_ak = [jnp.where(strict[s], m[s][:CHUNK, BW:], 0.0) for s in range(ns)]
        a_rb = [jnp.where(incl[s], m[s][CHUNK:, :BW], 0.0).astype(BF16) for s in range(ns)]
        a_rk = [jnp.where(incl[s], m[s][CHUNK:, BW:], 0.0) for s in range(ns)]
        avs = [_dot(jnp.concatenate([a_ak[s], a_rk[s]], axis=0).astype(BF16), bd(ld[s][6]))
               for s in range(ns)]
        tm = [eye + a for a in a_ab]
        pw = [_dot(a.astype(BF16), bd(a)) for a in a_ab]
        for _ in range(4):
            res = [_dot(jnp.concatenate([tm[s], pw[s]], axis=0).astype(BF16), bd(pw[s]))
                   for s in range(ns)]
            tm = [tm[s] + res[s][:CHUNK] for s in range(ns)]
            pw = [r[CHUNK:] for r in res]
        tm = [(tm[s] + _dot(tm[s].astype(BF16), bd(pw[s]))).astype(BF16) for s in range(ns)]
        p = [_dot(tm[s], bd(ld[s][0])) for s in range(ns)]
        q = [_dot(tm[s], bd(avs[s][:CHUNK])) for s in range(ns)]
        g = [ld[s][1].astype(F32) + _dot(a_rb[s], bd(p[s])) for s in range(ns)]
        y0 = [avs[s][CHUNK:] + _dot(a_rb[s], bd(q[s])) for s in range(ns)]
        hst = [h_ref[s] for s in range(ns)]
        gp = [_dot_nt(jnp.concatenate([g[s], p[s]], axis=0).astype(BF16), hst[s].astype(BF16))
              for s in range(ns)]
        for s, (d, b) in enumerate(seqs):
            y_ref = yf_ref if d == 0 else yb_ref
            y_ref[b, ld[s][8], :] = gp[s][:CHUNK] + y0[s]
        upd = [_dot_tn(jnp.concatenate([(gp[s][CHUNK:] + q[s]).astype(BF16), ld[s][6]], axis=0),
                       jnp.concatenate([ld[s][4], ld[s][5]], axis=0)) for s in range(ns)]
        for s in range(ns):
            h_ref[s] = hst[s] * ld[s][7] + upd[s] * bdm.astype(F32)
        return carry

    lax.fori_loop(0, n_chunk, chunk_body, 0)


def _rwkv_scan(ops, gc, bdm, *, tb):
    _, bsz, s_len, _ = ops[0].shape
    nb = s_len // tb
    nck = tb // CHUNK

    def specs(d):
        blk = (lambda i: i) if d == 0 else (lambda i: nb - 1 - i)
        big = pl.BlockSpec((None, bsz, tb, BW), lambda i: (d, 0, blk(i), 0))
        small = pl.BlockSpec((None, bsz, nck, BW), lambda i: (d, 0, blk(i), 0))
        return [big] * 7 + [small]

    out_f = pl.BlockSpec((bsz, tb, BW), lambda i: (0, i, 0))
    out_b = pl.BlockSpec((bsz, tb, BW), lambda i: (0, nb - 1 - i, 0))
    shape = jax.ShapeDtypeStruct((bsz, s_len, BW), F32)
    args = list(ops) + [gc]
    return pl.pallas_call(
        functools.partial(_rwkv_scan_kernel, n_chunk=nck, bsz=bsz),
        grid=(nb,),
        in_specs=specs(0) + specs(1) + [pl.BlockSpec((BW, BW), lambda i: (0, 0))],
        out_specs=[out_f, out_b],
        out_shape=[shape, shape],
        scratch_shapes=[pltpu.VMEM((2 * bsz, BW, BW), F32)],
        compiler_params=_cparams(("arbitrary",)),
        name="rwkv_scan",
    )(*args, *args, bdm)


def _mla_prep_kernel(qlo_ref, kvlo_ref, cs_ref, sn_ref, qn_ref, wq_ref, wqs_ref,
                     kvn_ref, wk_ref, wv_ref, pl_ref, pls_ref, q_out, k_out, v_out, *, scale):
    cs = jnp.concatenate([cs_ref[...]] * NH, axis=1)
    sn = jnp.concatenate([sn_ref[...]] * NH, axis=1)
    hq = _rms(qlo_ref[...], qn_ref[...]).astype(BF16)
    q = _dot(hq, wq_ref[...]) * cs + _dot(hq, wqs_ref[...]) * sn
    q_out[...] = (q * scale).astype(BF16)
    kv = kvlo_ref[...]
    hkv = _rms(kv[:, :KV_LORA], kvn_ref[...]).astype(BF16)
    kr = kv[:, KV_LORA:]
    k = (_dot(hkv, wk_ref[...]) + _dotb(kr, pl_ref[...])) * cs + _dotb(kr, pls_ref[...]) * sn
    k_out[...] = k.astype(BF16)
    v = _dot(hkv, wv_ref[...])
    lane = lax.broadcasted_iota(jnp.int32, v.shape, 1) % 128
    v_out[...] = jnp.where(lane == HD, 1.0, v).astype(BF16)


def _mla_prep(p2, cs, sn, qn, wq, wqs, kvn, wk, wv, plc, pls, *, ts):
    n = p2.shape[0]
    full = lambda shape: pl.BlockSpec(shape, lambda i: (0,) * len(shape))
    hw = NH * 128
    out = pl.BlockSpec((ts, hw), lambda i: (i, 0))
    shape = jax.ShapeDtypeStruct((n, hw), BF16)
    return pl.pallas_call(
        functools.partial(_mla_prep_kernel, scale=float((QK_NOPE + QK_ROPE) ** -0.5 * np.log2(np.e))),
        grid=(n // ts,),
        in_specs=[pl.BlockSpec((ts, 256), lambda i: (i, P_Q // 256)),
                  pl.BlockSpec((ts, 256), lambda i: (i, P_KV // 256)),
                  pl.BlockSpec((ts, 128), lambda i: (i, 0)),
                  pl.BlockSpec((ts, 128), lambda i: (i, 0)),
                  full((1, Q_LORA)), full((Q_LORA, hw)), full((Q_LORA, hw)),
                  full((1, KV_LORA)), full((KV_LORA, hw)), full((KV_LORA, hw)),
                  full((128, hw)), full((128, hw))],
        out_specs=[out] * 3,
        out_shape=[shape] * 3,
        compiler_params=_cparams(("parallel",)),
        name="mla_prep",
    )(p2, p2, cs, sn, qn, wq, wqs, kvn, wk, wv, plc, pls)


def _attn_kernel(q_ref, k_ref, v_ref, o_ref, *, rb):
    units = [(h, r) for h in range(NH) for r in range(q_ref.shape[0] // rb)]

    def scores(h, r):
        hs = slice(h * 128, (h + 1) * 128)
        return _dot_nt(q_ref[r * rb:(r + 1) * rb, hs], k_ref[:, hs])

    s_next = scores(*units[0])
    for idx, (h, r) in enumerate(units):
        s = s_next
        if idx + 1 < len(units):
            s_next = scores(*units[idx + 1])
        hs = slice(h * 128, (h + 1) * 128)
        p = jnp.exp2(s - jnp.max(s, axis=-1, keepdims=True)).astype(BF16)
        acc = _dot(p, v_ref[:, hs])
        o_ref[r * rb:(r + 1) * rb, hs] = (acc / acc[:, HD:HD + 1]).astype(o_ref.dtype)


def _attention(q, k, v, *, tq):
    bsz, s_len, hw = q.shape
    kv_spec = pl.BlockSpec((None, s_len, hw), lambda b, i: (b, 0, 0),
                           pipeline_mode=pl.Buffered(1))
    return pl.pallas_call(
        functools.partial(_attn_kernel, rb=_pick(tq, 256)),
        grid=(bsz, s_len // tq),
        in_specs=[pl.BlockSpec((None, tq, hw), lambda b, i: (b, i, 0)), kv_spec, kv_spec],
        out_specs=pl.BlockSpec((None, tq, hw), lambda b, i: (b, i, 0)),
        out_shape=jax.ShapeDtypeStruct((bsz, s_len, hw), BF16),
        compiler_params=_cparams(("parallel", "arbitrary")),
        name="mla_attention",
    )(q, k, v)


def _fft1_kernel(x_ref, c1_ref, s1n_ref, ar_out, ai_out, *, mb):
    c1 = c1_ref[...]
    s1n = s1n_ref[...]
    for m in range(mb):
        xm = x_ref[:, m, :].astype(BF16)
        ar_out[m] = _dot(c1, xm)
        ai_out[m] = _dot(s1n, xm)


def _fft1(p4, c1, s1n, *, mb):
    bsz, n1, n2, _ = p4.shape
    mat = pl.BlockSpec((n1, n1), lambda b, j: (0, 0))
    out = pl.BlockSpec((None, mb, n1, BW), lambda b, j: (b, j, 0, 0))
    shape = jax.ShapeDtypeStruct((bsz, n2, n1, BW), F32)
    return pl.pallas_call(
        functools.partial(_fft1_kernel, mb=mb),
        grid=(bsz, n2 // mb),
        in_specs=[pl.BlockSpec((None, n1, mb, BW), lambda b, j: (b, 0, j, P_F // BW)),
                  mat, mat],
        out_specs=[out, out],
        out_shape=[shape, shape],
        compiler_params=_cparams(("parallel", "parallel")),
        name="fft_stage1",
    )(p4, c1, s1n)


def _fft2_kernel(ar_ref, ai_ref, f_ref, g_ref, o_ref, *, kb):
    g = g_ref[...]
    for q in range(kb):
        ar = ar_ref[:, q, :]
        ai = ai_ref[:, q, :]
        rhs = jnp.concatenate([jnp.concatenate([ar, ai], axis=1),
                               jnp.concatenate([ai, -ar], axis=1)], axis=0).astype(BF16)
        uri = _dot(f_ref[q], rhs)
        o_ref[:, q, :] = _dot(uri.astype(BF16), g)


def _fft2(ar, ai, ftab, gtab, *, kb):
    bsz, n2, n1, _ = ar.shape
    blk = pl.BlockSpec((None, n2, kb, BW), lambda b, j: (b, 0, j, 0))
    return pl.pallas_call(
        functools.partial(_fft2_kernel, kb=kb),
        grid=(bsz, n1 // kb),
        in_specs=[blk, blk,
                  pl.BlockSpec((kb, n2, 2 * n2), lambda b, j: (j, 0, 0)),
                  pl.BlockSpec((2 * BW, BW), lambda b, j: (0, 0))],
        out_specs=blk,
        out_shape=jax.ShapeDtypeStruct((bsz, n2, n1, BW), F32),
        compiler_params=_cparams(("parallel", "parallel")),
        name="fft_stage2",
    )(ar, ai, ftab, gtab)


def _dft_mats(n):
    idx = np.arange(n)
    ang = 2.0 * np.pi * ((idx[:, None] * idx[None, :]) % n) / n
    return np.cos(ang), np.sin(ang)


def _fourier_constants(s_len):
    n2 = FFT_N2
    n1 = s_len // n2
    c1, s1 = _dft_mats(n1)
    k1 = np.arange(n1)[:, None, None]
    k2 = np.arange(n2)[None, :, None]
    m2 = np.arange(n2)[None, None, :]
    ang = 2.0 * np.pi * ((m2 * (n1 * k2 + k1)) % s_len) / s_len
    ftab = np.concatenate([np.cos(ang), np.sin(ang)], axis=2)
    cg64, sg64 = _dft_mats(HD)
    norm = 1.0 / np.sqrt(float(s_len) * HD)
    gtab = np.concatenate([np.kron(np.eye(NH), cg64), np.kron(np.eye(NH), sg64)], axis=0) * norm
    f = lambda a: jnp.asarray(a, BF16)
    return f(c1), f(-s1), f(ftab), f(gtab)


def _combine_kernel(x_ref, cv_ref, cvp_ref, cvn_ref, yf_ref, yb_ref, bonus_ref, rg_ref,
                    mla_ref, fn_ref, gn_ref, wg_ref, gb_ref, cw_ref, lng_ref, lnb_ref, avg_ref,
                    wa_ref, wb_ref, wc_ref, wd_ref, wo_ref, o_ref):
    i = pl.program_id(1)
    n_i = pl.num_programs(1)
    cv = cv_ref[...]
    cvp = cvp_ref[...]
    cvn = cvn_ref[...]
    z = cv[:, 2 * BW:] * cv[:, :BW]
    zp, zn = _shifted(z, cvp[:, 2 * BW:] * cvp[:, :BW], cvn[:, 2 * BW:] * cvn[:, :BW], i, n_i)
    conv = zp * cw_ref[0:1, :] + z * cw_ref[1:2, :] + zn * cw_ref[2:3, :]
    ya = _dot((cv[:, BW:2 * BW] * conv).astype(BF16), wa_ref[...])
    y = yf_ref[...] + yb_ref[...]
    avg = avg_ref[...]
    mean = _dot_exact_rhs(y, avg, 2)
    yc = y - mean
    var = _dot_exact_rhs(yc * yc, avg, 2)
    yn = yc * lax.rsqrt(var + RWKV_LN_EPS) * lng_ref[...] + lnb_ref[...]
    yb = _dot(((yn + bonus_ref[...]) * rg_ref[...]).astype(BF16), wb_ref[...])
    yc_ = _dot(mla_ref[...], wc_ref[...])
    yd = _dot(fn_ref[...].astype(BF16), wd_ref[...])
    x = x_ref[...]
    d = D_MODEL
    h = _rms(x, gn_ref[...]).astype(BF16)
    mix = None
    for br, yb_ in enumerate((ya, yb, yc_, yd)):
        cols = slice(br * d, (br + 1) * d)
        term = _sigmoid(_dot(h, wg_ref[:, cols]) + gb_ref[:, cols]) * yb_
        mix = term if mix is None else mix + term
    o_ref[...] = x + _dot(mix.astype(BF16), wo_ref[...])


def _combine(x3, p3, yf, yb, bonus, rg, mla_o, fn, gn, wg, gb, cw, lng, lnb, avg,
             wa, wb, wc, wd, wo, *, ts):
    bsz, s_len, d = x3.shape
    main, prev, nxt = _halo_specs(ts, 3 * BW, P_CONV // (3 * BW), s_len)
    row = lambda w: pl.BlockSpec((None, ts, w), lambda b, i: (b, i, 0))
    res = _resident
    return pl.pallas_call(
        _combine_kernel,
        grid=(bsz, s_len // ts),
        in_specs=[row(d), main, prev, nxt, row(BW), row(BW), row(BW), row(BW),
                  row(NH * 128), row(BW),
                  res((1, d)), res((d, N_BRANCH * d)), res((1, N_BRANCH * d)),
                  res((3, BW)), res((1, BW)), res((1, BW)), res((BW, BW)),
                  res((BW, d)), res((BW, d)), res((NH * 128, d)), res((BW, d)),
                  res((d, d))],
        out_specs=row(d),
        out_shape=jax.ShapeDtypeStruct(x3.shape, F32),
        compiler_params=_cparams(("parallel", "parallel")),
        name="combine",
    )(x3, p3, p3, p3, yf, yb, bonus, rg, mla_o, fn, gn, wg, gb, cw, lng, lnb, avg,
      wa, wb, wc, wd, wo)


def _ffn_kernel(x_ref, g_ref, wgu_ref, wd_ref, fg_ref, o_ref, *, final):
    x = x_ref[...]
    h = _rms(x, g_ref[...]).astype(BF16)
    gu = _dot(h, wgu_ref[...])
    gt = gu[:, :D_FF]
    act = (gt * _sigmoid(gt) * gu[:, D_FF:]).astype(BF16)
    y = x + _dot(act, wd_ref[...])
    if final:
        y = _rms(y, fg_ref[...])
    o_ref[...] = y


def _ffn(x, g, wgu, wd, fg, *, tm, final):
    n, d = x.shape
    return pl.pallas_call(
        functools.partial(_ffn_kernel, final=final),
        grid=(n // tm,),
        in_specs=[pl.BlockSpec((tm, d), lambda i: (i, 0)),
                  _resident((1, d)), _resident((d, 2 * D_FF)), _resident((D_FF, d)),
                  _resident((1, d))],
        out_specs=pl.BlockSpec((tm, d), lambda i: (i, 0)),
        out_shape=jax.ShapeDtypeStruct((n, d), F32),
        compiler_params=_cparams(("parallel",)),
        name="ffn",
    )(x, g, wgu, wd, fg)


def _head_cols(w, widths, total=128):
    k = w.shape[0]
    per = sum(widths)
    w = w.reshape(k, NH, per)
    w = jnp.pad(w, ((0, 0), (0, 0), (0, total - per)))
    return w.reshape(k, NH * total)


def _rope_swap_cols(w):
    k = w.shape[0]
    w = w.reshape(k, NH, 128)
    half = QK_ROPE // 2
    x1 = w[:, :, QK_NOPE:QK_NOPE + half]
    x2 = w[:, :, QK_NOPE + half:QK_NOPE + QK_ROPE]
    z = jnp.zeros_like(w)
    z = z.at[:, :, QK_NOPE:QK_NOPE + half].set(-x2)
    z = z.at[:, :, QK_NOPE + half:QK_NOPE + QK_ROPE].set(x1)
    return z.reshape(k, NH * 128)


def _pick(total, pref):
    t = min(total, pref)
    while total % t:
        t //= 2
    return t


def kernel(x, positions, mix_norm, w_in, gate_bias, conv_w, conv_out, rwkv_mu, rwkv_w0, rwkv_w_up, rwkv_a0, rwkv_a_up, rwkv_g_up, rwkv_k_k, rwkv_k_a, rwkv_r_k, rwkv_ln_g, rwkv_ln_b, rwkv_out, mla_q_norm, mla_w_uq, mla_kv_norm, mla_w_ukv, mla_out, fnet_out, w_o, ffn_norm, ffn_w_gu, ffn_w_down, final_norm):
    bsz, s_len, d = x.shape
    n = bsz * s_len
    depth = w_in.shape[0]
    n1 = s_len // FFT_N2

    inv_freq = ROPE_THETA ** (-jnp.arange(0, QK_ROPE, 2, dtype=F32) / QK_ROPE)
    ang = positions.astype(F32)[..., None] * inv_freq
    cos, sin = jnp.cos(ang), jnp.sin(ang)
    ones = jnp.ones((bsz, s_len, QK_NOPE), F32)
    zpad = jnp.zeros((bsz, s_len, 128 - QK_NOPE - QK_ROPE), F32)
    cs_tab = jnp.concatenate([ones, cos, cos, zpad], axis=-1).reshape(n, 128)
    sn_tab = jnp.concatenate([0 * ones, sin, sin, zpad], axis=-1).reshape(n, 128)

    c1, s1n, ftab, gtab = _fourier_constants(s_len)
    head_ones_np = np.kron(np.eye(NH), np.ones((HD, HD)))
    head_avg = jnp.asarray(head_ones_np / HD, BF16)
    head_mask = jnp.asarray(head_ones_np, BF16)

    place = np.zeros((128, NH * 128), np.float32)
    for h in range(NH):
        for jj in range(QK_ROPE):
            place[jj, h * 128 + QK_NOPE + jj] = 1.0
    place = jnp.asarray(place)
    place_sw = _rope_swap_cols(place)

    tm = _pick(n, 1024)
    ts = _pick(s_len, 512)
    tq = _pick(s_len, 1024)
    cuts = np.cumsum([768, 768, 384, 256, 160, 256])

    xf = x.reshape(n, d)
    for l in range(depth):
        w = w_in[l]
        w_small = jnp.concatenate(
            [w[:, :cuts[2]], jnp.zeros((d, P_Q - P_LORA - 384), F32),
             w[:, cuts[2]:cuts[4]], jnp.zeros((d, 256 - 160), F32),
             w[:, cuts[4]:cuts[5]], jnp.zeros((d, P_W - P_F - 256), F32)], axis=1).astype(BF16)
        w_gate = w[:, cuts[5]:].astype(BF16)
        g_mix = mix_norm[l].reshape(1, d)
        p2 = _norm_mm(xf, g_mix, w_small, tm=_pick(n, 512))
        p3 = p2.reshape(bsz, s_len, P_W)

        *scan_ops, gc, bonus, rg = _rwkv_prep(
            p3, rwkv_mu[l].reshape(2, 3 * BW), rwkv_w0[l], rwkv_w_up[l], rwkv_a0[l],
            rwkv_a_up[l], rwkv_g_up[l], rwkv_k_k[l].reshape(1, BW), rwkv_k_a[l].reshape(1, BW),
            rwkv_r_k[l].reshape(1, BW), head_mask, ts=ts)
        yf, yb = _rwkv_scan(scan_ops, gc, head_mask, tb=ts)

        wq = _head_cols(mla_w_uq[l], (QK_NOPE, QK_ROPE))
        wkv = mla_w_ukv[l].reshape(KV_LORA, NH, 2 * HD)
        wk = _head_cols(wkv[:, :, :HD].reshape(KV_LORA, NH * HD), (HD,))
        wv = _head_cols(wkv[:, :, HD:].reshape(KV_LORA, NH * HD), (HD,))
        q, k, v = _mla_prep(p2, cs_tab, sn_tab, mla_q_norm[l].reshape(1, Q_LORA),
                            wq.astype(BF16), _rope_swap_cols(wq).astype(BF16),
                            mla_kv_norm[l].reshape(1, KV_LORA), wk.astype(BF16),
                            wv.astype(BF16), place, place_sw, ts=ts)
        hw = NH * 128
        mla_o = _attention(q.reshape(bsz, s_len, hw), k.reshape(bsz, s_len, hw),
                           v.reshape(bsz, s_len, hw), tq=tq)
        wc = jnp.pad(mla_out[l].reshape(NH, HD, d), ((0, 0), (0, 128 - HD), (0, 0))).reshape(hw, d)

        ar, ai = _fft1(p2.reshape(bsz, n1, FFT_N2, P_W), c1, s1n, mb=16)
        fn = _fft2(ar, ai, ftab, gtab, kb=_pick(n1, 8)).reshape(bsz, s_len, BW)

        x3 = _combine(xf.reshape(bsz, s_len, d), p3, yf, yb, bonus, rg, mla_o, fn,
                      g_mix, w_gate, gate_bias[l].reshape(1, N_BRANCH * d), conv_w[l],
                      rwkv_ln_g[l].reshape(1, BW), rwkv_ln_b[l].reshape(1, BW), head_avg,
                      conv_out[l].astype(BF16), rwkv_out[l].astype(BF16), wc.astype(BF16),
                      fnet_out[l].astype(BF16), w_o[l].astype(BF16), ts=ts)
        xf = _ffn(x3.reshape(n, d), ffn_norm[l].reshape(1, d), ffn_w_gu[l].astype(BF16),
                  ffn_w_down[l].astype(BF16), final_norm.reshape(1, d),
                  tm=_pick(n, 512), final=(l == depth - 1))
    return xf.reshape(bsz, s_len, d)
```

```python
import functools

import numpy as np
import jax
import jax.numpy as jnp
from jax import lax
from jax.experimental import pallas as pl
from jax.experimental.pallas import tpu as pltpu

F32 = jnp.float32
BF16 = jnp.bfloat16

D_MODEL = 1024
N_BRANCH = 4
BW = 256
HD = 64
NH = BW // HD
LORA = 64
GATE_LORA = 128
Q_LORA = 256
KV_LORA = 128
QK_NOPE = 64
QK_ROPE = 32
D_FF = 2816
NORM_EPS = 1e-6
RWKV_LN_EPS = 64e-5
ROPE_THETA = 10000.0
CHUNK = 64
FFT_N2 = 128

P_CONV = 0
P_RKV = 768
P_LORA = 1536
P_Q = 2048
P_KV = 2304
P_F = 2560
P_W = 2816

VMEM_LIMIT = 56 * 1024 * 1024


def _cparams(sem):
    return pltpu.CompilerParams(dimension_semantics=sem, vmem_limit_bytes=VMEM_LIMIT)


def _resident(shape):
    return pl.BlockSpec(shape, lambda *_: (0,) * len(shape), pipeline_mode=pl.Buffered(1))


def _layer(shape, l):
    return pl.BlockSpec((None,) + shape, lambda *_: (l,) + (0,) * len(shape),
                        pipeline_mode=pl.Buffered(1))


def _dot(a, b, prec=None):
    return jnp.dot(a, b, preferred_element_type=F32, precision=prec)


def _dot_nt(a, b, prec=None):
    return lax.dot_general(a, b, (((1,), (1,)), ((), ())),
                           preferred_element_type=F32, precision=prec)


def _dot_tn(a, b, prec=None):
    return lax.dot_general(a, b, (((0,), (0,)), ((), ())),
                           preferred_element_type=F32, precision=prec)


def _dotb(a, b):
    return _dot(a.astype(BF16), b.astype(BF16))


def _bf16_terms(x, terms):
    parts = []
    for _ in range(terms):
        p = x.astype(BF16)
        parts.append(p)
        x = x - p.astype(F32)
    return parts


def _dot_exact_rhs(x, m01, terms):
    return sum(_dot(p, m01) for p in _bf16_terms(x, terms))


def _rms(xf, g, eps=NORM_EPS):
    return xf * lax.rsqrt(jnp.mean(xf * xf, axis=-1, keepdims=True) + eps) * g


def _sigmoid(z):
    return 1.0 / (1.0 + jnp.exp(-z))


def _norm_mm_kernel(x_ref, g_ref, w_ref, o_ref):
    h = _rms(x_ref[...], g_ref[...]).astype(BF16)
    o_ref[...] = _dot(h, w_ref[...])


def _norm_mm(x, g, w, l, *, tm):
    n, d = x.shape
    nc = w.shape[2]
    return pl.pallas_call(
        _norm_mm_kernel,
        grid=(n // tm,),
        in_specs=[pl.BlockSpec((tm, d), lambda i: (i, 0)), _resident((1, d)),
                  _layer((d, nc), l)],
        out_specs=pl.BlockSpec((tm, nc), lambda i: (i, 0)),
        out_shape=jax.ShapeDtypeStruct((n, nc), F32),
        compiler_params=_cparams(("parallel",)),
        name="norm_mm",
    )(x, g, w)


def _shifted(t, prev_blk, next_blk, i, n_i):
    ts = t.shape[0]
    prev_row = jnp.where(i == 0, 0.0, prev_blk[7:8, :])
    next_row = jnp.where(i == n_i - 1, 0.0, next_blk[0:1, :])
    rows = lax.broadcasted_iota(jnp.int32, (ts, 1), 0)
    t_prev = jnp.where(rows == 0, prev_row, pltpu.roll(t, 1, axis=0))
    t_next = jnp.where(rows == ts - 1, next_row, pltpu.roll(t, ts - 1, axis=0))
    return t_prev, t_next


def _halo_specs(ts, width, col_blk, s_len):
    r8 = ts // 8
    last8 = s_len // 8 - 1
    main = pl.BlockSpec((None, ts, width), lambda b, i: (b, i, col_blk))
    prev = pl.BlockSpec((None, 8, width),
                        lambda b, i: (b, jnp.maximum(i * r8 - 1, 0), col_blk))
    nxt = pl.BlockSpec((None, 8, width),
                       lambda b, i: (b, jnp.minimum((i + 1) * r8, last8), col_blk))
    return main, prev, nxt


def _softplus(z):
    return jnp.maximum(z, 0.0) + jnp.log(1.0 + jnp.exp(-jnp.abs(z)))


def _rwkv_prep_kernel(rkv_ref, prev_ref, next_ref, lora_ref, mu_ref, w0_ref, wup_ref,
                      a0_ref, aup_ref, gup_ref, kk_ref, ka_ref, rk_ref, bd_ref,
                      at_out, rt_out, bt_out, kt_out, bh_out, kh_out, v_out, gc_out,
                      bonus_out, g_out):
    i = pl.program_id(1)
    n_i = pl.num_programs(1)
    t = rkv_ref[...]
    ts = t.shape[0]
    nck = ts // CHUNK
    t_prev, t_next = _shifted(t, prev_ref[...], next_ref[...], i, n_i)
    lora = lora_ref[...]
    bd = bd_ref[...]
    bonus = None
    per_dir = []
    for d in range(2):
        sh = t_prev if d == 0 else t_next
        mixed = t + mu_ref[d:d + 1, :] * (sh - t)
        rd = mixed[:, 0:BW]
        kd = mixed[:, BW:2 * BW]
        vd = mixed[:, 2 * BW:3 * BW]
        w_l = jnp.tanh(lora[:, d * LORA:(d + 1) * LORA])
        a_l = lora[:, 2 * LORA + d * LORA:2 * LORA + (d + 1) * LORA]
        w_pre = w0_ref[d:d + 1, :] + _dotb(w_l, wup_ref[d])
        w_log = -_softplus(-w_pre) - 0.5
        lw = -jnp.exp(w_log)
        a = _sigmoid(a0_ref[d:d + 1, :] + _dotb(a_l, aup_ref[d]))
        kk = kd * kk_ref[...]
        ss = _dot_exact_rhs(kk * kk, bd, 2)
        kk = kk / jnp.maximum(jnp.sqrt(ss), 1e-12)
        kt = kd * (1.0 + (a - 1.0) * ka_ref[...])
        bo = _dot_exact_rhs(rd * kt * rk_ref[...], bd, 2) * vd
        bonus = bo if bonus is None else bonus + bo
        per_dir.append((rd, kt, vd, kk, kk * a, lw))
    bonus_out[...] = bonus
    g_out[...] = _dotb(_sigmoid(lora[:, 4 * LORA:4 * LORA + GATE_LORA]), gup_ref[...])

    rows = lax.broadcasted_iota(jnp.int32, (ts, ts), 0)
    cols = lax.broadcasted_iota(jnp.int32, (ts, ts), 1)
    tri = jnp.where((rows // CHUNK == cols // CHUNK) & (cols <= rows), 1.0, 0.0).astype(BF16)
    lw_terms = _bf16_terms(jnp.concatenate([per_dir[0][5], per_dir[1][5]], axis=1), 3)
    pre = sum(_dot(tri, p) for p in lw_terms)
    for d in range(2):
        rd, kt, vd, kk, b, lw = per_dir[d]
        tot_c = jnp.sum(lw.reshape(nck, CHUNK, BW), axis=1)
        tot = jnp.broadcast_to(tot_c[:, None, :], (nck, CHUNK, BW)).reshape(ts, BW)
        p = pre[:, d * BW:(d + 1) * BW]
        cum = p if d == 0 else tot - p + lw
        g_inv = jnp.exp(-cum)
        g_rem = jnp.exp(tot - cum)
        at_out[d] = (-kk * jnp.exp(cum - lw)).astype(BF16)
        rt_out[d] = (rd * jnp.exp(cum)).astype(BF16)
        bt_out[d] = (b * g_inv).astype(BF16)
        kt_out[d] = (kt * g_inv).astype(BF16)
        bh_out[d] = (b * g_rem).astype(BF16)
        kh_out[d] = (kt * g_rem).astype(BF16)
        v_out[d] = vd.astype(BF16)
        gc_out[d] = jnp.exp(tot_c)


def _rwkv_prep(p3, mu, w0, wup, a0, aup, gup, k_k, k_a, r_k, bd, *, ts):
    bsz, s_len, _ = p3.shape
    nck = ts // CHUNK
    main, prev, nxt = _halo_specs(ts, 3 * BW, P_RKV // (3 * BW), s_len)
    lora = pl.BlockSpec((None, ts, 384), lambda b, i: (b, i, P_LORA // 384))
    full = lambda shape: pl.BlockSpec(shape, lambda b, i: (0,) * len(shape))
    dir_out = pl.BlockSpec((2, None, ts, BW), lambda b, i: (0, b, i, 0))
    gc_out = pl.BlockSpec((2, None, nck, BW), lambda b, i: (0, b, i, 0))
    one_out = pl.BlockSpec((None, ts, BW), lambda b, i: (b, i, 0))
    dir_shape = jax.ShapeDtypeStruct((2, bsz, s_len, BW), BF16)
    gc_shape = jax.ShapeDtypeStruct((2, bsz, s_len // CHUNK, BW), F32)
    one_shape = jax.ShapeDtypeStruct((bsz, s_len, BW), F32)
    return pl.pallas_call(
        _rwkv_prep_kernel,
        grid=(bsz, s_len // ts),
        in_specs=[main, prev, nxt, lora,
                  full((2, 3 * BW)), full((2, BW)), full((2, LORA, BW)),
                  full((2, BW)), full((2, LORA, BW)), full((GATE_LORA, BW)),
                  full((1, BW)), full((1, BW)), full((1, BW)), full((BW, BW))],
        out_specs=[dir_out] * 7 + [gc_out] + [one_out] * 2,
        out_shape=[dir_shape] * 7 + [gc_shape] + [one_shape] * 2,
        compiler_params=_cparams(("parallel", "parallel")),
        name="rwkv_prep",
    )(p3, p3, p3, p3, mu, w0, wup, a0, aup, gup, k_k, k_a, r_k, bd)


def _rwkv_scan_kernel(*refs, n_chunk, bsz):
    n_in = 8
    fwd = refs[0:n_in]
    bwd = refs[n_in:2 * n_in]
    bdm_ref = refs[2 * n_in]
    yf_ref, yb_ref, h_ref = refs[2 * n_in + 1:]

    @pl.when(pl.program_id(0) == 0)
    def _():
        h_ref[...] = jnp.zeros_like(h_ref)

    bdm = bdm_ref[...]
    rows = lax.broadcasted_iota(jnp.int32, (CHUNK, BW), 0)
    cols = lax.broadcasted_iota(jnp.int32, (CHUNK, BW), 1) % CHUNK
    eye = jnp.where(rows == cols, 1.0, 0.0).astype(F32)

    def bd(x):
        xb = x.astype(BF16)
        return jnp.concatenate([xb] * NH, axis=0) * bdm

    def chunk_body(c2, carry):
        seqs = [(d, b) for d in range(2) for b in range(bsz)]
        nq = len(seqs)
        streams = [(par, d, b) for par in range(2) for d, b in seqs]
        ns = len(streams)
        ld = []
        for par, d, b in streams:
            in_refs = fwd if d == 0 else bwd
            c = 2 * c2 + par
            cidx = c if d == 0 else n_chunk - 1 - c
            sl = pl.ds(pl.multiple_of(cidx * CHUNK, CHUNK), CHUNK)
            ld.append([r[b, sl, :] for r in in_refs[:7]]
                      + [in_refs[7][b, pl.ds(cidx, 1), :], sl])
        strict = [(rows > cols) if d == 0 else (rows < cols) for _, d, _ in streams]
        incl = [(rows >= cols) if d == 0 else (rows <= cols) for _, d, _ in streams]
        m = [_dot_nt(jnp.concatenate([x[0], x[1]], axis=0),
                     jnp.concatenate([jnp.concatenate([x[2]] * NH, axis=0) * bdm,
                                      jnp.concatenate([x[3]] * NH, axis=0) * bdm], axis=0))
             for x in ld]
        a_ab = [jnp.where(strict[s], m[s][:CHUNK, :BW], 0.0) for s in range(ns)]
        a_ak = [jnp.where(strict[s], m[s][:CHUNK, BW:], 0.0) for s in range(ns)]
        a_rb = [jnp.where(incl[s], m[s][CHUNK:, :BW], 0.0).astype(BF16) for s in range(ns)]
        a_rk = [jnp.where(incl[s], m[s][CHUNK:, BW:], 0.0) for s in range(ns)]
        avs = [_dot(jnp.concatenate([a_ak[s], a_rk[s]], axis=0).astype(BF16), bd(ld[s][6]))
               for s in range(ns)]
        tm = [eye + a for a in a_ab]
        pw = [_dot(a.astype(BF16), bd(a)) for a in a_ab]
        for _ in range(4):
            res = [_dot(jnp.concatenate([tm[s], pw[s]], axis=0).astype(BF16), bd(pw[s]))
                   for s in range(ns)]
            tm = [tm[s] + res[s][:CHUNK] for s in range(ns)]
            pw = [r[CHUNK:] for r in res]
        tm = [(tm[s] + _dot(tm[s].astype(BF16), bd(pw[s]))).astype(BF16) for s in range(ns)]
        p = [_dot(tm[s], bd(ld[s][0])) for s in range(ns)]
        q = [_dot(tm[s], bd(avs[s][:CHUNK])) for s in range(ns)]
        g = [ld[s][1].astype(F32) + _dot(a_rb[s], bd(p[s])) for s in range(ns)]
        y0 = [avs[s][CHUNK:] + _dot(a_rb[s], bd(q[s])) for s in range(ns)]
        gpl = [jnp.concatenate([g[s], p[s]], axis=0).astype(BF16) for s in range(ns)]
        hst = [h_ref[i] for i in range(nq)]
        for par in range(2):
            st = [par * nq + i for i in range(nq)]
            gp = [_dot_nt(gpl[s], hst[i].astype(BF16)) for i, s in enumerate(st)]
            for i, s in enumerate(st):
                _, d, b = streams[s]
                y_ref = yf_ref if d == 0 else yb_ref
                y_ref[b, ld[s][8], :] = gp[i][:CHUNK] + y0[s]
            upd = [_dot_tn(jnp.concatenate([(gp[i][CHUNK:] + q[s]).astype(BF16), ld[s][6]],
                                           axis=0),
                           jnp.concatenate([ld[s][4], ld[s][5]], axis=0))
                   for i, s in enumerate(st)]
            hst = [hst[i] * ld[s][7] + upd[i] * bdm.astype(F32) for i, s in enumerate(st)]
        for i in range(nq):
            h_ref[i] = hst[i]
        return carry

    lax.fori_loop(0, n_chunk // 2, chunk_body, 0)


def _rwkv_scan(ops, gc, bdm, *, tb):
    _, bsz, s_len, _ = ops[0].shape
    nb = s_len // tb
    nck = tb // CHUNK

    def specs(d):
        blk = (lambda i: i) if d == 0 else (lambda i: nb - 1 - i)
        big = pl.BlockSpec((None, bsz, tb, BW), lambda i: (d, 0, blk(i), 0))
        small = pl.BlockSpec((None, bsz, nck, BW), lambda i: (d, 0, blk(i), 0))
        return [big] * 7 + [small]

    out_f = pl.BlockSpec((bsz, tb, BW), lambda i: (0, i, 0))
    out_b = pl.BlockSpec((bsz, tb, BW), lambda i: (0, nb - 1 - i, 0))
    shape = jax.ShapeDtypeStruct((bsz, s_len, BW), F32)
    args = list(ops) + [gc]
    return pl.pallas_call(
        functools.partial(_rwkv_scan_kernel, n_chunk=nck, bsz=bsz),
        grid=(nb,),
        in_specs=specs(0) + specs(1) + [pl.BlockSpec((BW, BW), lambda i: (0, 0))],
        out_specs=[out_f, out_b],
        out_shape=[shape, shape],
        scratch_shapes=[pltpu.VMEM((2 * bsz, BW, BW), F32)],
        compiler_params=_cparams(("arbitrary",)),
        name="rwkv_scan",
    )(*args, *args, bdm)


def _mla_prep_kernel(qlo_ref, kvlo_ref, cs_ref, sn_ref, qn_ref, wq_ref, wqs_ref,
                     kvn_ref, wk_ref, wv_ref, pl_ref, pls_ref, q_out, k_out, v_out, *, scale):
    cs = jnp.concatenate([cs_ref[...]] * NH, axis=1)
    sn = jnp.concatenate([sn_ref[...]] * NH, axis=1)
    hq = _rms(qlo_ref[...], qn_ref[...]).astype(BF16)
    q = _dot(hq, wq_ref[...]) * cs + _dot(hq, wqs_ref[...]) * sn
    q_out[...] = (q * scale).astype(BF16)
    kv = kvlo_ref[...]
    hkv = _rms(kv[:, :KV_LORA], kvn_ref[...]).astype(BF16)
    kr = kv[:, KV_LORA:]
    k = (_dot(hkv, wk_ref[...]) + _dotb(kr, pl_ref[...])) * cs + _dotb(kr, pls_ref[...]) * sn
    k_out[...] = k.astype(BF16)
    v = _dot(hkv, wv_ref[...])
    lane = lax.broadcasted_iota(jnp.int32, v.shape, 1) % 128
    v_out[...] = jnp.where(lane == HD, 1.0, v).astype(BF16)


def _mla_prep(p2, cs, sn, qn, wq, wqs, kvn, wk, wv, plc, pls, *, ts):
    n = p2.shape[0]
    full = lambda shape: pl.BlockSpec(shape, lambda i: (0,) * len(shape))
    hw = NH * 128
    out = pl.BlockSpec((ts, hw), lambda i: (i, 0))
    shape = jax.ShapeDtypeStruct((n, hw), BF16)
    return pl.pallas_call(
        functools.partial(_mla_prep_kernel, scale=float((QK_NOPE + QK_ROPE) ** -0.5 * np.log2(np.e))),
        grid=(n // ts,),
        in_specs=[pl.BlockSpec((ts, 256), lambda i: (i, P_Q // 256)),
                  pl.BlockSpec((ts, 256), lambda i: (i, P_KV // 256)),
                  pl.BlockSpec((ts, 128), lambda i: (i, 0)),
                  pl.BlockSpec((ts, 128), lambda i: (i, 0)),
                  full((1, Q_LORA)), full((Q_LORA, hw)), full((Q_LORA, hw)),
                  full((1, KV_LORA)), full((KV_LORA, hw)), full((KV_LORA, hw)),
                  full((128, hw)), full((128, hw))],
        out_specs=[out] * 3,
        out_shape=[shape] * 3,
        compiler_params=_cparams(("parallel",)),
        name="mla_prep",
    )(p2, p2, cs, sn, qn, wq, wqs, kvn, wk, wv, plc, pls)


def _attn_kernel(q_ref, k_ref, v_ref, o_ref, m_ref, acc_ref, *, rb):
    j = pl.program_id(2)

    @pl.when(j == 0)
    def _():
        m_ref[...] = jnp.full_like(m_ref, -jnp.inf)
        acc_ref[...] = jnp.zeros_like(acc_ref)

    units = [(h, r) for h in range(NH) for r in range(q_ref.shape[0] // rb)]

    def scores(h, r):
        hs = slice(h * 128, (h + 1) * 128)
        return _dot_nt(q_ref[r * rb:(r + 1) * rb, hs], k_ref[:, hs])

    s_next = scores(*units[0])
    for idx, (h, r) in enumerate(units):
        s = s_next
        if idx + 1 < len(units):
            s_next = scores(*units[idx + 1])
        rows = slice(r * rb, (r + 1) * rb)
        m_old = m_ref[h, rows]
        m_new = jnp.maximum(m_old, jnp.max(s, axis=-1, keepdims=True))
        p = jnp.exp2(s - m_new).astype(BF16)
        alpha = jnp.exp2(m_old - m_new)
        acc_ref[h, rows] = alpha * acc_ref[h, rows] + _dot(p, v_ref[:, h * 128:(h + 1) * 128])
        m_ref[h, rows] = m_new

    @pl.when(j == pl.num_programs(2) - 1)
    def _():
        for h in range(NH):
            acc = acc_ref[h]
            o_ref[:, h * 128:(h + 1) * 128] = (acc / acc[:, HD:HD + 1]).astype(o_ref.dtype)


def _attention(q, k, v, *, tq, tk):
    bsz, s_len, hw = q.shape
    return pl.pallas_call(
        functools.partial(_attn_kernel, rb=_pick(tq, 256)),
        grid=(bsz, s_len // tq, s_len // tk),
        in_specs=[pl.BlockSpec((None, tq, hw), lambda b, i, j: (b, i, 0)),
                  pl.BlockSpec((None, tk, hw), lambda b, i, j: (b, j, 0)),
                  pl.BlockSpec((None, tk, hw), lambda b, i, j: (b, j, 0))],
        out_specs=pl.BlockSpec((None, tq, hw), lambda b, i, j: (b, i, 0)),
        out_shape=jax.ShapeDtypeStruct((bsz, s_len, hw), BF16),
        scratch_shapes=[pltpu.VMEM((NH, tq, 1), F32), pltpu.VMEM((NH, tq, 128), F32)],
        compiler_params=_cparams(("parallel", "parallel", "arbitrary")),
        name="mla_attention",
    )(q, k, v)


def _fft1_kernel(x_ref, c1_ref, s1n_ref, ar_out, ai_out, *, mb):
    c1 = c1_ref[...]
    s1n = s1n_ref[...]
    for m in range(mb):
        xm = x_ref[:, m, :].astype(BF16)
        ar_out[m] = _dot(c1, xm)
        ai_out[m] = _dot(s1n, xm)


def _fft1(p4, c1, s1n, *, mb):
    bsz, n1, n2, _ = p4.shape
    mat = pl.BlockSpec((n1, n1), lambda b, j: (0, 0))
    out = pl.BlockSpec((None, mb, n1, BW), lambda b, j: (b, j, 0, 0))
    shape = jax.ShapeDtypeStruct((bsz, n2, n1, BW), F32)
    return pl.pallas_call(
        functools.partial(_fft1_kernel, mb=mb),
        grid=(bsz, n2 // mb),
        in_specs=[pl.BlockSpec((None, n1, mb, BW), lambda b, j: (b, 0, j, P_F // BW)),
                  mat, mat],
        out_specs=[out, out],
        out_shape=[shape, shape],
        compiler_params=_cparams(("parallel", "parallel")),
        name="fft_stage1",
    )(p4, c1, s1n)


def _fft2_kernel(ar_ref, ai_ref, f_ref, g_ref, o_ref, *, kb):
    g = g_ref[...]
    for q in range(kb):
        ar = ar_ref[:, q, :]
        ai = ai_ref[:, q, :]
        rhs = jnp.concatenate([jnp.concatenate([ar, ai], axis=1),
                               jnp.concatenate([ai, -ar], axis=1)], axis=0).astype(BF16)
        uri = _dot(f_ref[q], rhs)
        o_ref[:, q, :] = _dot(uri.astype(BF16), g)


def _fft2(ar, ai, ftab, gtab, *, kb):
    bsz, n2, n1, _ = ar.shape
    blk = pl.BlockSpec((None, n2, kb, BW), lambda b, j: (b, 0, j, 0))
    return pl.pallas_call(
        functools.partial(_fft2_kernel, kb=kb),
        grid=(bsz, n1 // kb),
        in_specs=[blk, blk,
                  pl.BlockSpec((kb, n2, 2 * n2), lambda b, j: (j, 0, 0)),
                  pl.BlockSpec((2 * BW, BW), lambda b, j: (0, 0))],
        out_specs=blk,
        out_shape=jax.ShapeDtypeStruct((bsz, n2, n1, BW), F32),
        compiler_params=_cparams(("parallel", "parallel")),
        name="fft_stage2",
    )(ar, ai, ftab, gtab)


def _dft_mats(n):
    idx = np.arange(n)
    ang = 2.0 * np.pi * ((idx[:, None] * idx[None, :]) % n) / n
    return np.cos(ang), np.sin(ang)


def _fourier_constants(s_len):
    n2 = FFT_N2
    n1 = s_len // n2
    c1, s1 = _dft_mats(n1)
    k1 = np.arange(n1)[:, None, None]
    k2 = np.arange(n2)[None, :, None]
    m2 = np.arange(n2)[None, None, :]
    ang = 2.0 * np.pi * ((m2 * (n1 * k2 + k1)) % s_len) / s_len
    ftab = np.concatenate([np.cos(ang), np.sin(ang)], axis=2)
    cg64, sg64 = _dft_mats(HD)
    norm = 1.0 / np.sqrt(float(s_len) * HD)
    gtab = np.concatenate([np.kron(np.eye(NH), cg64), np.kron(np.eye(NH), sg64)], axis=0) * norm
    f = lambda a: jnp.asarray(a, BF16)
    return f(c1), f(-s1), f(ftab), f(gtab)


def _combine_kernel(x_ref, cv_ref, cvp_ref, cvn_ref, yf_ref, yb_ref, bonus_ref, rg_ref,
                    mla_ref, fn_ref, gn_ref, wg_ref, gb_ref, cw_ref, lng_ref, lnb_ref, avg_ref,
                    wa_ref, wb_ref, wc_ref, wd_ref, wo_ref, o_ref):
    i = pl.program_id(1)
    n_i = pl.num_programs(1)
    cv = cv_ref[...]
    cvp = cvp_ref[...]
    cvn = cvn_ref[...]
    z = cv[:, 2 * BW:] * cv[:, :BW]
    zp, zn = _shifted(z, cvp[:, 2 * BW:] * cvp[:, :BW], cvn[:, 2 * BW:] * cvn[:, :BW], i, n_i)
    conv = zp * cw_ref[0:1, :] + z * cw_ref[1:2, :] + zn * cw_ref[2:3, :]
    ya = _dot((cv[:, BW:2 * BW] * conv).astype(BF16), wa_ref[...])
    y = yf_ref[...] + yb_ref[...]
    avg = avg_ref[...]
    mean = _dot_exact_rhs(y, avg, 2)
    yc = y - mean
    var = _dot_exact_rhs(yc * yc, avg, 2)
    yn = yc * lax.rsqrt(var + RWKV_LN_EPS) * lng_ref[...] + lnb_ref[...]
    yb = _dot(((yn + bonus_ref[...]) * rg_ref[...]).astype(BF16), wb_ref[...])
    yc_ = _dot(mla_ref[...], wc_ref[...])
    yd = _dot(fn_ref[...].astype(BF16), wd_ref[...])
    x = x_ref[...]
    d = D_MODEL
    h = _rms(x, gn_ref[...]).astype(BF16)
    mix = None
    for br, yb_ in enumerate((ya, yb, yc_, yd)):
        cols = slice(br * d, (br + 1) * d)
        term = _sigmoid(_dot(h, wg_ref[:, cols]) + gb_ref[:, cols]) * yb_
        mix = term if mix is None else mix + term
    o_ref[...] = x + _dot(mix.astype(BF16), wo_ref[...])


def _combine(x3, p3, yf, yb, bonus, rg, mla_o, fn, gn, wg, gb, cw, lng, lnb, avg,
             wa, wb, wc, wd, wo, l, *, ts):
    bsz, s_len, d = x3.shape
    main, prev, nxt = _halo_specs(ts, 3 * BW, P_CONV // (3 * BW), s_len)
    row = lambda w: pl.BlockSpec((None, ts, w), lambda b, i: (b, i, 0))
    res = _resident
    lay = lambda shape: _layer(shape, l)
    return pl.pallas_call(
        _combine_kernel,
        grid=(bsz, s_len // ts),
        in_specs=[row(d), main, prev, nxt, row(BW), row(BW), row(BW), row(BW),
                  row(NH * 128), row(BW),
                  res((1, d)), lay((d, N_BRANCH * d)), res((1, N_BRANCH * d)),
                  res((3, BW)), res((1, BW)), res((1, BW)), res((BW, BW)),
                  lay((BW, d)), lay((BW, d)), lay((NH * 128, d)), lay((BW, d)),
                  lay((d, d))],
        out_specs=row(d),
        out_shape=jax.ShapeDtypeStruct(x3.shape, F32),
        compiler_params=_cparams(("parallel", "parallel")),
        name="combine",
    )(x3, p3, p3, p3, yf, yb, bonus, rg, mla_o, fn, gn, wg, gb, cw, lng, lnb, avg,
      wa, wb, wc, wd, wo)


def _ffn_kernel(x_ref, g_ref, wgu_ref, wd_ref, fg_ref, o_ref, *, final):
    x = x_ref[...]
    h = _rms(x, g_ref[...]).astype(BF16)
    gu = _dot(h, wgu_ref[...])
    gt = gu[:, :D_FF]
    act = (gt * _sigmoid(gt) * gu[:, D_FF:]).astype(BF16)
    y = x + _dot(act, wd_ref[...])
    if final:
        y = _rms(y, fg_ref[...])
    o_ref[...] = y


def _ffn(x, g, wgu, wd, fg, l, *, tm, final):
    n, d = x.shape
    return pl.pallas_call(
        functools.partial(_ffn_kernel, final=final),
        grid=(n // tm,),
        in_specs=[pl.BlockSpec((tm, d), lambda i: (i, 0)),
                  _resident((1, d)), _layer((d, 2 * D_FF), l), _layer((D_FF, d), l),
                  _resident((1, d))],
        out_specs=pl.BlockSpec((tm, d), lambda i: (i, 0)),
        out_shape=jax.ShapeDtypeStruct((n, d), F32),
        compiler_params=_cparams(("parallel",)),
        name="ffn",
    )(x, g, wgu, wd, fg)


def _head_cols(w, widths, total=128):
    k = w.shape[0]
    per = sum(widths)
    w = w.reshape(k, NH, per)
    w = jnp.pad(w, ((0, 0), (0, 0), (0, total - per)))
    return w.reshape(k, NH * total)


def _rope_swap_cols(w):
    k = w.shape[0]
    w = w.reshape(k, NH, 128)
    half = QK_ROPE // 2
    x1 = w[:, :, QK_NOPE:QK_NOPE + half]
    x2 = w[:, :, QK_NOPE + half:QK_NOPE + QK_ROPE]
    z = jnp.zeros_like(w)
    z = z.at[:, :, QK_NOPE:QK_NOPE + half].set(-x2)
    z = z.at[:, :, QK_NOPE + half:QK_NOPE + QK_ROPE].set(x1)
    return z.reshape(k, NH * 128)


def _pick(total, pref):
    t = min(total, pref)
    while total % t:
        t //= 2
    return t


def kernel(x, positions, mix_norm, w_in, gate_bias, conv_w, conv_out, rwkv_mu, rwkv_w0, rwkv_w_up, rwkv_a0, rwkv_a_up, rwkv_g_up, rwkv_k_k, rwkv_k_a, rwkv_r_k, rwkv_ln_g, rwkv_ln_b, rwkv_out, mla_q_norm, mla_w_uq, mla_kv_norm, mla_w_ukv, mla_out, fnet_out, w_o, ffn_norm, ffn_w_gu, ffn_w_down, final_norm):
    bsz, s_len, d = x.shape
    n = bsz * s_len
    depth = w_in.shape[0]
    n1 = s_len // FFT_N2

    inv_freq = ROPE_THETA ** (-jnp.arange(0, QK_ROPE, 2, dtype=F32) / QK_ROPE)
    ang = (positions.astype(F32)[..., None] * inv_freq).reshape(-1, 128)
    half = (bsz, s_len, QK_ROPE // 2)
    cos, sin = jnp.cos(ang).reshape(half), jnp.sin(ang).reshape(half)
    ones = jnp.ones((bsz, s_len, QK_NOPE), F32)
    zpad = jnp.zeros((bsz, s_len, 128 - QK_NOPE - QK_ROPE), F32)
    cs_tab = jnp.concatenate([ones, cos, cos, zpad], axis=-1).reshape(n, 128)
    sn_tab = jnp.concatenate([0 * ones, sin, sin, zpad], axis=-1).reshape(n, 128)

    c1, s1n, ftab, gtab = _fourier_constants(s_len)
    head_ones_np = np.kron(np.eye(NH), np.ones((HD, HD)))
    head_avg = jnp.asarray(head_ones_np / HD, BF16)
    head_mask = jnp.asarray(head_ones_np, BF16)

    place = np.zeros((128, NH * 128), np.float32)
    for h in range(NH):
        for jj in range(QK_ROPE):
            place[jj, h * 128 + QK_NOPE + jj] = 1.0
    place = jnp.asarray(place)
    place_sw = _rope_swap_cols(place)

    tm = _pick(n, 1024)
    ts = _pick(s_len, 512)
    tq = _pick(s_len, 1024)
    cuts = np.cumsum([768, 768, 384, 256, 160, 256])

    zcols = lambda k: jnp.zeros((depth, d, k), BF16)
    wb16 = w_in.astype(BF16)
    w_small = jnp.concatenate(
        [wb16[:, :, :cuts[2]], zcols(P_Q - P_LORA - 384), wb16[:, :, cuts[2]:cuts[4]],
         zcols(256 - 160), wb16[:, :, cuts[4]:cuts[5]], zcols(P_W - P_F - 256)], axis=2)
    w_gate = wb16[:, :, cuts[5]:]
    wa_all, wb_all, wd_all, wo_all = (t.astype(BF16) for t in (conv_out, rwkv_out, fnet_out, w_o))
    wc_all = jnp.pad(mla_out.astype(BF16).reshape(depth, NH, HD, d),
                     ((0, 0), (0, 0), (0, 128 - HD), (0, 0))).reshape(depth, NH * 128, d)
    wgu_all = ffn_w_gu.astype(BF16)
    wdn_all = ffn_w_down.astype(BF16)

    xf = x.reshape(n, d)
    for l in range(depth):
        g_mix = mix_norm[l].reshape(1, d)
        p2 = _norm_mm(xf, g_mix, w_small, l, tm=_pick(n, 512))
        p3 = p2.reshape(bsz, s_len, P_W)

        *scan_ops, gc, bonus, rg = _rwkv_prep(
            p3, rwkv_mu[l].reshape(2, 3 * BW), rwkv_w0[l], rwkv_w_up[l], rwkv_a0[l],
            rwkv_a_up[l], rwkv_g_up[l], rwkv_k_k[l].reshape(1, BW), rwkv_k_a[l].reshape(1, BW),
            rwkv_r_k[l].reshape(1, BW), head_mask, ts=ts)
        yf, yb = _rwkv_scan(scan_ops, gc, head_mask, tb=ts)

        wq = _head_cols(mla_w_uq[l], (QK_NOPE, QK_ROPE))
        wkv = mla_w_ukv[l].reshape(KV_LORA, NH, 2 * HD)
        wk = _head_cols(wkv[:, :, :HD].reshape(KV_LORA, NH * HD), (HD,))
        wv = _head_cols(wkv[:, :, HD:].reshape(KV_LORA, NH * HD), (HD,))
        q, k, v = _mla_prep(p2, cs_tab, sn_tab, mla_q_norm[l].reshape(1, Q_LORA),
                            wq.astype(BF16), _rope_swap_cols(wq).astype(BF16),
                            mla_kv_norm[l].reshape(1, KV_LORA), wk.astype(BF16),
                            wv.astype(BF16), place, place_sw, ts=ts)
        hw = NH * 128
        mla_o = _attention(q.reshape(bsz, s_len, hw), k.reshape(bsz, s_len, hw),
                           v.reshape(bsz, s_len, hw), tq=tq, tk=_pick(s_len, 2048))

        ar, ai = _fft1(p2.reshape(bsz, n1, FFT_N2, P_W), c1, s1n, mb=16)
        fn = _fft2(ar, ai, ftab, gtab, kb=_pick(n1, 8)).reshape(bsz, s_len, BW)

        x3 = _combine(xf.reshape(bsz, s_len, d), p3, yf, yb, bonus, rg, mla_o, fn,
                      g_mix, w_gate, gate_bias[l].reshape(1, N_BRANCH * d), conv_w[l],
                      rwkv_ln_g[l].reshape(1, BW), rwkv_ln_b[l].reshape(1, BW), head_avg,
                      wa_all, wb_all, wc_all, wd_all, wo_all, l, ts=ts)
        xf = _ffn(x3.reshape(n, d), ffn_norm[l].reshape(1, d), wgu_all, wdn_all,
                  final_norm.reshape(1, d), l, tm=_pick(n, 512), final=(l == depth - 1))
    return xf.reshape(bsz, s_len, d)
```

```python
import functools

import numpy as np
import jax
import jax.numpy as jnp
from jax import lax
from jax.experimental import pallas as pl
from jax.experimental.pallas import tpu as pltpu

F32 = jnp.float32
BF16 = jnp.bfloat16

D_MODEL = 1024
N_BRANCH = 4
BW = 256
HD = 64
NH = BW // HD
LORA = 64
GATE_LORA = 128
Q_LORA = 256
KV_LORA = 128
QK_NOPE = 64
QK_ROPE = 32
D_FF = 2816
NORM_EPS = 1e-6
RWKV_LN_EPS = 64e-5
ROPE_THETA = 10000.0
CHUNK = 64
FFT_N2 = 128

P_CONV = 0
P_RKV = 768
P_LORA = 1536
P_Q = 2048
P_KV = 2304
P_F = 2560
P_W = 2816

VMEM_LIMIT = 56 * 1024 * 1024


def _cparams(sem):
    return pltpu.CompilerParams(dimension_semantics=sem, vmem_limit_bytes=VMEM_LIMIT)


def _resident(shape):
    return pl.BlockSpec(shape, lambda *_: (0,) * len(shape), pipeline_mode=pl.Buffered(1))


def _layer(shape, l):
    return pl.BlockSpec((None,) + shape, lambda *_: (l,) + (0,) * len(shape),
                        pipeline_mode=pl.Buffered(1))


def _dot(a, b, prec=None):
    return jnp.dot(a, b, preferred_element_type=F32, precision=prec)


def _dot_nt(a, b, prec=None):
    return lax.dot_general(a, b, (((1,), (1,)), ((), ())),
                           preferred_element_type=F32, precision=prec)


def _dot_tn(a, b, prec=None):
    return lax.dot_general(a, b, (((0,), (0,)), ((), ())),
                           preferred_element_type=F32, precision=prec)


def _dotb(a, b):
    return _dot(a.astype(BF16), b.astype(BF16))


def _bf16_terms(x, terms):
    parts = []
    for _ in range(terms):
        p = x.astype(BF16)
        parts.append(p)
        x = x - p.astype(F32)
    return parts


def _dot_exact_rhs(x, m01, terms):
    return sum(_dot(p, m01) for p in _bf16_terms(x, terms))


def _rms(xf, g, eps=NORM_EPS):
    return xf * lax.rsqrt(jnp.mean(xf * xf, axis=-1, keepdims=True) + eps) * g


def _sigmoid(z):
    return 1.0 / (1.0 + jnp.exp(-z))


def _norm_mm_kernel(x_ref, g_ref, w_ref, o_ref):
    h = _rms(x_ref[...], g_ref[...]).astype(BF16)
    o_ref[...] = _dot(h, w_ref[...])


def _norm_mm(x, g, w, l, *, tm):
    n, d = x.shape
    nc = w.shape[2]
    return pl.pallas_call(
        _norm_mm_kernel,
        grid=(n // tm,),
        in_specs=[pl.BlockSpec((tm, d), lambda i: (i, 0)), _resident((1, d)),
                  _layer((d, nc), l)],
        out_specs=pl.BlockSpec((tm, nc), lambda i: (i, 0)),
        out_shape=jax.ShapeDtypeStruct((n, nc), F32),
        compiler_params=_cparams(("parallel",)),
        name="norm_mm",
    )(x, g, w)


def _shifted(t, prev_blk, next_blk, i, n_i):
    ts = t.shape[0]
    prev_row = jnp.where(i == 0, 0.0, prev_blk[7:8, :])
    next_row = jnp.where(i == n_i - 1, 0.0, next_blk[0:1, :])
    rows = lax.broadcasted_iota(jnp.int32, (ts, 1), 0)
    t_prev = jnp.where(rows == 0, prev_row, pltpu.roll(t, 1, axis=0))
    t_next = jnp.where(rows == ts - 1, next_row, pltpu.roll(t, ts - 1, axis=0))
    return t_prev, t_next


def _halo_specs(ts, width, col_blk, s_len):
    r8 = ts // 8
    last8 = s_len // 8 - 1
    main = pl.BlockSpec((None, ts, width), lambda b, i: (b, i, col_blk))
    prev = pl.BlockSpec((None, 8, width),
                        lambda b, i: (b, jnp.maximum(i * r8 - 1, 0), col_blk))
    nxt = pl.BlockSpec((None, 8, width),
                       lambda b, i: (b, jnp.minimum((i + 1) * r8, last8), col_blk))
    return main, prev, nxt


def _softplus(z):
    return jnp.maximum(z, 0.0) + jnp.log(1.0 + jnp.exp(-jnp.abs(z)))


def _rwkv_prep_kernel(rkv_ref, prev_ref, next_ref, lora_ref, mu_ref, w0_ref, wup_ref,
                      a0_ref, aup_ref, gup_ref, kk_ref, ka_ref, rk_ref, bd_ref,
                      at_out, rt_out, bt_out, kt_out, bh_out, kh_out, v_out, gc_out,
                      bonus_out, g_out):
    i = pl.program_id(1)
    n_i = pl.num_programs(1)
    t = rkv_ref[...]
    ts = t.shape[0]
    nck = ts // CHUNK
    t_prev, t_next = _shifted(t, prev_ref[...], next_ref[...], i, n_i)
    lora = lora_ref[...]
    bd = bd_ref[...]
    bonus = None
    per_dir = []
    for d in range(2):
        sh = t_prev if d == 0 else t_next
        mixed = t + mu_ref[d:d + 1, :] * (sh - t)
        rd = mixed[:, 0:BW]
        kd = mixed[:, BW:2 * BW]
        vd = mixed[:, 2 * BW:3 * BW]
        w_l = jnp.tanh(lora[:, d * LORA:(d + 1) * LORA])
        a_l = lora[:, 2 * LORA + d * LORA:2 * LORA + (d + 1) * LORA]
        w_pre = w0_ref[d:d + 1, :] + _dotb(w_l, wup_ref[d])
        w_log = -_softplus(-w_pre) - 0.5
        lw = -jnp.exp(w_log)
        a = _sigmoid(a0_ref[d:d + 1, :] + _dotb(a_l, aup_ref[d]))
        kk = kd * kk_ref[...]
        ss = _dot_exact_rhs(kk * kk, bd, 2)
        kk = kk / jnp.maximum(jnp.sqrt(ss), 1e-12)
        kt = kd * (1.0 + (a - 1.0) * ka_ref[...])
        bo = _dot_exact_rhs(rd * kt * rk_ref[...], bd, 2) * vd
        bonus = bo if bonus is None else bonus + bo
        per_dir.append((rd, kt, vd, kk, kk * a, lw))
    bonus_out[...] = bonus
    g_out[...] = _dotb(_sigmoid(lora[:, 4 * LORA:4 * LORA + GATE_LORA]), gup_ref[...])

    rows = lax.broadcasted_iota(jnp.int32, (ts, ts), 0)
    cols = lax.broadcasted_iota(jnp.int32, (ts, ts), 1)
    tri = jnp.where((rows // CHUNK == cols // CHUNK) & (cols <= rows), 1.0, 0.0).astype(BF16)
    lw_terms = _bf16_terms(jnp.concatenate([per_dir[0][5], per_dir[1][5]], axis=1), 3)
    pre = sum(_dot(tri, p) for p in lw_terms)
    for d in range(2):
        rd, kt, vd, kk, b, lw = per_dir[d]
        tot_c = jnp.sum(lw.reshape(nck, CHUNK, BW), axis=1)
        tot = jnp.broadcast_to(tot_c[:, None, :], (nck, CHUNK, BW)).reshape(ts, BW)
        p = pre[:, d * BW:(d + 1) * BW]
        cum = p if d == 0 else tot - p + lw
        g_inv = jnp.exp(-cum)
        g_rem = jnp.exp(tot - cum)
        at_out[d] = (-kk * jnp.exp(cum - lw)).astype(BF16)
        rt_out[d] = (rd * jnp.exp(cum)).astype(BF16)
        bt_out[d] = (b * g_inv).astype(BF16)
        kt_out[d] = (kt * g_inv).astype(BF16)
        bh_out[d] = (b * g_rem).astype(BF16)
        kh_out[d] = (kt * g_rem).astype(BF16)
        v_out[d] = vd.astype(BF16)
        gc_out[d] = jnp.exp(tot_c)


def _rwkv_prep(p3, mu, w0, wup, a0, aup, gup, k_k, k_a, r_k, bd, *, ts):
    bsz, s_len, _ = p3.shape
    nck = ts // CHUNK
    main, prev, nxt = _halo_specs(ts, 3 * BW, P_RKV // (3 * BW), s_len)
    lora = pl.BlockSpec((None, ts, 384), lambda b, i: (b, i, P_LORA // 384))
    full = lambda shape: pl.BlockSpec(shape, lambda b, i: (0,) * len(shape))
    dir_out = pl.BlockSpec((2, None, ts, BW), lambda b, i: (0, b, i, 0))
    gc_out = pl.BlockSpec((2, None, nck, BW), lambda b, i: (0, b, i, 0))
    one_out = pl.BlockSpec((None, ts, BW), lambda b, i: (b, i, 0))
    dir_shape = jax.ShapeDtypeStruct((2, bsz, s_len, BW), BF16)
    gc_shape = jax.ShapeDtypeStruct((2, bsz, s_len // CHUNK, BW), F32)
    one_shape = jax.ShapeDtypeStruct((bsz, s_len, BW), F32)
    return pl.pallas_call(
        _rwkv_prep_kernel,
        grid=(bsz, s_len // ts),
        in_specs=[main, prev, nxt, lora,
                  full((2, 3 * BW)), full((2, BW)), full((2, LORA, BW)),
                  full((2, BW)), full((2, LORA, BW)), full((GATE_LORA, BW)),
                  full((1, BW)), full((1, BW)), full((1, BW)), full((BW, BW))],
        out_specs=[dir_out] * 7 + [gc_out] + [one_out] * 2,
        out_shape=[dir_shape] * 7 + [gc_shape] + [one_shape] * 2,
        compiler_params=_cparams(("parallel", "parallel")),
        name="rwkv_prep",
    )(p3, p3, p3, p3, mu, w0, wup, a0, aup, gup, k_k, k_a, r_k, bd)


def _rwkv_scan_kernel(*refs, n_chunk, bsz):
    n_in = 8
    fwd = refs[0:n_in]
    bwd = refs[n_in:2 * n_in]
    bdm_ref = refs[2 * n_in]
    yf_ref, yb_ref, h_ref = refs[2 * n_in + 1:]

    @pl.when(pl.program_id(0) == 0)
    def _():
        h_ref[...] = jnp.zeros_like(h_ref)

    bdm = bdm_ref[...]
    rows = lax.broadcasted_iota(jnp.int32, (CHUNK, BW), 0)
    cols = lax.broadcasted_iota(jnp.int32, (CHUNK, BW), 1) % CHUNK
    eye = jnp.where(rows == cols, 1.0, 0.0).astype(F32)

    def bd(x):
        xb = x.astype(BF16)
        return jnp.concatenate([xb] * NH, axis=0) * bdm

    def chunk_body(c2, carry):
        seqs = [(d, b) for d in range(2) for b in range(bsz)]
        nq = len(seqs)
        streams = [(par, d, b) for par in range(2) for d, b in seqs]
        ns = len(streams)
        ld = []
        for par, d, b in streams:
            in_refs = fwd if d == 0 else bwd
            c = 2 * c2 + par
            cidx = c if d == 0 else n_chunk - 1 - c
            sl = pl.ds(pl.multiple_of(cidx * CHUNK, CHUNK), CHUNK)
            ld.append([r[b, sl, :] for r in in_refs[:7]]
                      + [in_refs[7][b, pl.ds(cidx, 1), :], sl])
        strict = [(rows > cols) if d == 0 else (rows < cols) for _, d, _ in streams]
        incl = [(rows >= cols) if d == 0 else (rows <= cols) for _, d, _ in streams]
        m = [_dot_nt(jnp.concatenate([x[0], x[1]], axis=0),
                     jnp.concatenate([jnp.concatenate([x[2]] * NH, axis=0) * bdm,
                                      jnp.concatenate([x[3]] * NH, axis=0) * bdm], axis=0))
             for x in ld]
        a_ab = [jnp.where(strict[s], m[s][:CHUNK, :BW], 0.0) for s in range(ns)]
        a_ak = [jnp.where(strict[s], m[s][:CHUNK, BW:], 0.0) for s in range(ns)]
        a_rb = [jnp.where(incl[s], m[s][CHUNK:, :BW], 0.0).astype(BF16) for s in range(ns)]
        a_rk = [jnp.where(incl[s], m[s][CHUNK:, BW:], 0.0) for s in range(ns)]
        avs = [_dot(jnp.concatenate([a_ak[s], a_rk[s]], axis=0).astype(BF16), bd(ld[s][6]))
               for s in range(ns)]
        tm = [eye + a for a in a_ab]
        pw = [_dot(a.astype(BF16), bd(a)) for a in a_ab]
        for _ in range(4):
            res = [_dot(jnp.concatenate([tm[s], pw[s]], axis=0).astype(BF16), bd(pw[s]))
                   for s in range(ns)]
            tm = [tm[s] + res[s][:CHUNK] for s in range(ns)]
            pw = [r[CHUNK:] for r in res]
        tm = [(tm[s] + _dot(tm[s].astype(BF16), bd(pw[s]))).astype(BF16) for s in range(ns)]
        p = [_dot(tm[s], bd(ld[s][0])) for s in range(ns)]
        q = [_dot(tm[s], bd(avs[s][:CHUNK])) for s in range(ns)]
        g = [ld[s][1].astype(F32) + _dot(a_rb[s], bd(p[s])) for s in range(ns)]
        y0 = [avs[s][CHUNK:] + _dot(a_rb[s], bd(q[s])) for s in range(ns)]
        gpl = [jnp.concatenate([g[s], p[s]], axis=0).astype(BF16) for s in range(ns)]
        hst = [h_ref[i] for i in range(nq)]
        for par in range(2):
            st = [par * nq + i for i in range(nq)]
            gp = [_dot_nt(gpl[s], hst[i].astype(BF16)) for i, s in enumerate(st)]
            for i, s in enumerate(st):
                _, d, b = streams[s]
                y_ref = yf_ref if d == 0 else yb_ref
                y_ref[b, ld[s][8], :] = gp[i][:CHUNK] + y0[s]
            upd = [_dot_tn(jnp.concatenate([(gp[i][CHUNK:] + q[s]).astype(BF16), ld[s][6]],
                                           axis=0),
                           jnp.concatenate([ld[s][4], ld[s][5]], axis=0))
                   for i, s in enumerate(st)]
            hst = [hst[i] * ld[s][7] + upd[i] * bdm.astype(F32) for i, s in enumerate(st)]
        for i in range(nq):
            h_ref[i] = hst[i]
        return carry

    lax.fori_loop(0, n_chunk // 2, chunk_body, 0)


def _rwkv_scan(ops, gc, bdm, *, tb):
    _, bsz, s_len, _ = ops[0].shape
    nb = s_len // tb
    nck = tb // CHUNK

    def specs(d):
        blk = (lambda i: i) if d == 0 else (lambda i: nb - 1 - i)
        big = pl.BlockSpec((None, bsz, tb, BW), lambda i: (d, 0, blk(i), 0))
        small = pl.BlockSpec((None, bsz, nck, BW), lambda i: (d, 0, blk(i), 0))
        return [big] * 7 + [small]

    out_f = pl.BlockSpec((bsz, tb, BW), lambda i: (0, i, 0))
    out_b = pl.BlockSpec((bsz, tb, BW), lambda i: (0, nb - 1 - i, 0))
    shape = jax.ShapeDtypeStruct((bsz, s_len, BW), F32)
    args = list(ops) + [gc]
    return pl.pallas_call(
        functools.partial(_rwkv_scan_kernel, n_chunk=nck, bsz=bsz),
        grid=(nb,),
        in_specs=specs(0) + specs(1) + [pl.BlockSpec((BW, BW), lambda i: (0, 0))],
        out_specs=[out_f, out_b],
        out_shape=[shape, shape],
        scratch_shapes=[pltpu.VMEM((2 * bsz, BW, BW), F32)],
        compiler_params=_cparams(("arbitrary",)),
        name="rwkv_scan",
    )(*args, *args, bdm)


def _mla_prep_kernel(qlo_ref, kvlo_ref, cs_ref, sn_ref, qn_ref, wq_ref, wqs_ref,
                     kvn_ref, wk_ref, wv_ref, pl_ref, pls_ref, q_out, k_out, v_out, *, scale):
    cs = jnp.concatenate([cs_ref[...]] * NH, axis=1)
    sn = jnp.concatenate([sn_ref[...]] * NH, axis=1)
    hq = _rms(qlo_ref[...], qn_ref[...]).astype(BF16)
    q = _dot(hq, wq_ref[...]) * cs + _dot(hq, wqs_ref[...]) * sn
    q_out[...] = (q * scale).astype(BF16)
    kv = kvlo_ref[...]
    hkv = _rms(kv[:, :KV_LORA], kvn_ref[...]).astype(BF16)
    kr = kv[:, KV_LORA:]
    k = (_dot(hkv, wk_ref[...]) + _dotb(kr, pl_ref[...])) * cs + _dotb(kr, pls_ref[...]) * sn
    k_out[...] = k.astype(BF16)
    v = _dot(hkv, wv_ref[...])
    lane = lax.broadcasted_iota(jnp.int32, v.shape, 1) % 128
    v_out[...] = jnp.where(lane == HD, 1.0, v).astype(BF16)


def _mla_prep(p2, cs, sn, qn, wq, wqs, kvn, wk, wv, plc, pls, *, ts):
    n = p2.shape[0]
    full = lambda shape: pl.BlockSpec(shape, lambda i: (0,) * len(shape))
    hw = NH * 128
    out = pl.BlockSpec((ts, hw), lambda i: (i, 0))
    shape = jax.ShapeDtypeStruct((n, hw), BF16)
    return pl.pallas_call(
        functools.partial(_mla_prep_kernel, scale=float((QK_NOPE + QK_ROPE) ** -0.5 * np.log2(np.e))),
        grid=(n // ts,),
        in_specs=[pl.BlockSpec((ts, 256), lambda i: (i, P_Q // 256)),
                  pl.BlockSpec((ts, 256), lambda i: (i, P_KV // 256)),
                  pl.BlockSpec((ts, 128), lambda i: (i, 0)),
                  pl.BlockSpec((ts, 128), lambda i: (i, 0)),
                  full((1, Q_LORA)), full((Q_LORA, hw)), full((Q_LORA, hw)),
                  full((1, KV_LORA)), full((KV_LORA, hw)), full((KV_LORA, hw)),
                  full((128, hw)), full((128, hw))],
        out_specs=[out] * 3,
        out_shape=[shape] * 3,
        compiler_params=_cparams(("parallel",)),
        name="mla_prep",
    )(p2, p2, cs, sn, qn, wq, wqs, kvn, wk, wv, plc, pls)


def _attn_kernel(q_ref, k_ref, v_ref, o_ref, m_ref, acc_ref, *, rb):
    j = pl.program_id(2)

    @pl.when(j == 0)
    def _():
        m_ref[...] = jnp.full_like(m_ref, -jnp.inf)
        acc_ref[...] = jnp.zeros_like(acc_ref)

    units = [(h, r) for h in range(NH) for r in range(q_ref.shape[0] // rb)]

    def scores(h, r):
        hs = slice(h * 128, (h + 1) * 128)
        return _dot_nt(q_ref[r * rb:(r + 1) * rb, hs], k_ref[:, hs])

    s_next = scores(*units[0])
    for idx, (h, r) in enumerate(units):
        s = s_next
        if idx + 1 < len(units):
            s_next = scores(*units[idx + 1])
        rows = slice(r * rb, (r + 1) * rb)
        m_old = m_ref[h, rows]
        m_new = jnp.maximum(m_old, jnp.max(s, axis=-1, keepdims=True))
        p = jnp.exp2(s - m_new).astype(BF16)
        alpha = jnp.exp2(m_old - m_new)
        acc_ref[h, rows] = alpha * acc_ref[h, rows] + _dot(p, v_ref[:, h * 128:(h + 1) * 128])
        m_ref[h, rows] = m_new

    @pl.when(j == pl.num_programs(2) - 1)
    def _():
        outs = []
        for h in range(NH):
            acc = acc_ref[h]
            outs.append(acc[:, :HD] / acc[:, HD:HD + 1])
        o_ref[...] = jnp.concatenate(outs, axis=1).astype(o_ref.dtype)


def _attention(q, k, v, *, tq, tk):
    bsz, s_len, hw = q.shape
    return pl.pallas_call(
        functools.partial(_attn_kernel, rb=_pick(tq, 256)),
        grid=(bsz, s_len // tq, s_len // tk),
        in_specs=[pl.BlockSpec((None, tq, hw), lambda b, i, j: (b, i, 0)),
                  pl.BlockSpec((None, tk, hw), lambda b, i, j: (b, j, 0)),
                  pl.BlockSpec((None, tk, hw), lambda b, i, j: (b, j, 0))],
        out_specs=pl.BlockSpec((None, tq, BW), lambda b, i, j: (b, i, 0)),
        out_shape=jax.ShapeDtypeStruct((bsz, s_len, BW), BF16),
        scratch_shapes=[pltpu.VMEM((NH, tq, 1), F32), pltpu.VMEM((NH, tq, 128), F32)],
        compiler_params=_cparams(("parallel", "parallel", "arbitrary")),
        name="mla_attention",
    )(q, k, v)


def _fft1_kernel(x_ref, c1_ref, s1n_ref, ar_out, ai_out, *, mb):
    c1 = c1_ref[...]
    s1n = s1n_ref[...]
    for m in range(mb):
        xm = x_ref[:, m, :].astype(BF16)
        ar_out[m] = _dot(c1, xm)
        ai_out[m] = _dot(s1n, xm)


def _fft1(p4, c1, s1n, *, mb):
    bsz, n1, n2, _ = p4.shape
    mat = pl.BlockSpec((n1, n1), lambda b, j: (0, 0))
    out = pl.BlockSpec((None, mb, n1, BW), lambda b, j: (b, j, 0, 0))
    shape = jax.ShapeDtypeStruct((bsz, n2, n1, BW), F32)
    return pl.pallas_call(
        functools.partial(_fft1_kernel, mb=mb),
        grid=(bsz, n2 // mb),
        in_specs=[pl.BlockSpec((None, n1, mb, BW), lambda b, j: (b, 0, j, P_F // BW)),
                  mat, mat],
        out_specs=[out, out],
        out_shape=[shape, shape],
        compiler_params=_cparams(("parallel", "parallel")),
        name="fft_stage1",
    )(p4, c1, s1n)


def _fft2_kernel(ar_ref, ai_ref, f_ref, g_ref, o_ref, *, kb):
    g = g_ref[...]
    for q in range(kb):
        ar = ar_ref[:, q, :]
        ai = ai_ref[:, q, :]
        rhs = jnp.concatenate([jnp.concatenate([ar, ai], axis=1),
                               jnp.concatenate([ai, -ar], axis=1)], axis=0).astype(BF16)
        uri = _dot(f_ref[q], rhs)
        o_ref[:, q, :] = _dot(uri.astype(BF16), g)


def _fft2(ar, ai, ftab, gtab, *, kb):
    bsz, n2, n1, _ = ar.shape
    blk = pl.BlockSpec((None, n2, kb, BW), lambda b, j: (b, 0, j, 0))
    return pl.pallas_call(
        functools.partial(_fft2_kernel, kb=kb),
        grid=(bsz, n1 // kb),
        in_specs=[blk, blk,
                  pl.BlockSpec((kb, n2, 2 * n2), lambda b, j: (j, 0, 0)),
                  pl.BlockSpec((2 * BW, BW), lambda b, j: (0, 0))],
        out_specs=blk,
        out_shape=jax.ShapeDtypeStruct((bsz, n2, n1, BW), F32),
        compiler_params=_cparams(("parallel", "parallel")),
        name="fft_stage2",
    )(ar, ai, ftab, gtab)


def _dft_mats(n):
    idx = np.arange(n)
    ang = 2.0 * np.pi * ((idx[:, None] * idx[None, :]) % n) / n
    return np.cos(ang), np.sin(ang)


def _fourier_constants(s_len):
    n2 = FFT_N2
    n1 = s_len // n2
    c1, s1 = _dft_mats(n1)
    k1 = np.arange(n1)[:, None, None]
    k2 = np.arange(n2)[None, :, None]
    m2 = np.arange(n2)[None, None, :]
    ang = 2.0 * np.pi * ((m2 * (n1 * k2 + k1)) % s_len) / s_len
    ftab = np.concatenate([np.cos(ang), np.sin(ang)], axis=2)
    cg64, sg64 = _dft_mats(HD)
    norm = 1.0 / np.sqrt(float(s_len) * HD)
    gtab = np.concatenate([np.kron(np.eye(NH), cg64), np.kron(np.eye(NH), sg64)], axis=0) * norm
    f = lambda a: jnp.asarray(a, BF16)
    return f(c1), f(-s1), f(ftab), f(gtab)


def _combine_kernel(x_ref, cv_ref, cvp_ref, cvn_ref, yf_ref, yb_ref, bonus_ref, rg_ref,
                    mla_ref, fn_ref, gn_ref, wg_ref, gb_ref, cw_ref, lng_ref, lnb_ref, avg_ref,
                    wa_ref, wb_ref, wc_ref, wd_ref, wo_ref, o_ref):
    i = pl.program_id(1)
    n_i = pl.num_programs(1)
    cv = cv_ref[...]
    cvp = cvp_ref[...]
    cvn = cvn_ref[...]
    z = cv[:, 2 * BW:] * cv[:, :BW]
    zp, zn = _shifted(z, cvp[:, 2 * BW:] * cvp[:, :BW], cvn[:, 2 * BW:] * cvn[:, :BW], i, n_i)
    conv = zp * cw_ref[0:1, :] + z * cw_ref[1:2, :] + zn * cw_ref[2:3, :]
    ya = _dot((cv[:, BW:2 * BW] * conv).astype(BF16), wa_ref[...])
    y = yf_ref[...] + yb_ref[...]
    avg = avg_ref[...]
    mean = _dot_exact_rhs(y, avg, 2)
    yc = y - mean
    var = _dot_exact_rhs(yc * yc, avg, 2)
    yn = yc * lax.rsqrt(var + RWKV_LN_EPS) * lng_ref[...] + lnb_ref[...]
    yb = _dot(((yn + bonus_ref[...]) * rg_ref[...]).astype(BF16), wb_ref[...])
    yc_ = _dot(mla_ref[...], wc_ref[...])
    yd = _dot(fn_ref[...].astype(BF16), wd_ref[...])
    x = x_ref[...]
    d = D_MODEL
    h = _rms(x, gn_ref[...]).astype(BF16)
    mix = None
    for br, yb_ in enumerate((ya, yb, yc_, yd)):
        cols = slice(br * d, (br + 1) * d)
        term = _sigmoid(_dot(h, wg_ref[:, cols]) + gb_ref[:, cols]) * yb_
        mix = term if mix is None else mix + term
    o_ref[...] = x + _dot(mix.astype(BF16), wo_ref[...])


def _combine(x3, p3, yf, yb, bonus, rg, mla_o, fn, gn, wg, gb, cw, lng, lnb, avg,
             wa, wb, wc, wd, wo, l, *, ts):
    bsz, s_len, d = x3.shape
    main, prev, nxt = _halo_specs(ts, 3 * BW, P_CONV // (3 * BW), s_len)
    row = lambda w: pl.BlockSpec((None, ts, w), lambda b, i: (b, i, 0))
    res = _resident
    lay = lambda shape: _layer(shape, l)
    return pl.pallas_call(
        _combine_kernel,
        grid=(bsz, s_len // ts),
        in_specs=[row(d), main, prev, nxt, row(BW), row(BW), row(BW), row(BW),
                  row(BW), row(BW),
                  res((1, d)), lay((d, N_BRANCH * d)), res((1, N_BRANCH * d)),
                  res((3, BW)), res((1, BW)), res((1, BW)), res((BW, BW)),
                  lay((BW, d)), lay((BW, d)), lay((BW, d)), lay((BW, d)),
                  lay((d, d))],
        out_specs=row(d),
        out_shape=jax.ShapeDtypeStruct(x3.shape, F32),
        compiler_params=_cparams(("parallel", "parallel")),
        name="combine",
    )(x3, p3, p3, p3, yf, yb, bonus, rg, mla_o, fn, gn, wg, gb, cw, lng, lnb, avg,
      wa, wb, wc, wd, wo)


def _ffn_kernel(x_ref, g_ref, wgu_ref, wd_ref, fg_ref, o_ref, *, final):
    x = x_ref[...]
    h = _rms(x, g_ref[...]).astype(BF16)
    gu = _dot(h, wgu_ref[...])
    gt = gu[:, :D_FF]
    act = (gt * _sigmoid(gt) * gu[:, D_FF:]).astype(BF16)
    y = x + _dot(act, wd_ref[...])
    if final:
        y = _rms(y, fg_ref[...])
    o_ref[...] = y


def _ffn(x, g, wgu, wd, fg, l, *, tm, final):
    n, d = x.shape
    return pl.pallas_call(
        functools.partial(_ffn_kernel, final=final),
        grid=(n // tm,),
        in_specs=[pl.BlockSpec((tm, d), lambda i: (i, 0)),
                  _resident((1, d)), _layer((d, 2 * D_FF), l), _layer((D_FF, d), l),
                  _resident((1, d))],
        out_specs=pl.BlockSpec((tm, d), lambda i: (i, 0)),
        out_shape=jax.ShapeDtypeStruct((n, d), F32),
        compiler_params=_cparams(("parallel",)),
        name="ffn",
    )(x, g, wgu, wd, fg)


def _head_cols(w, widths, total=128):
    k = w.shape[0]
    per = sum(widths)
    w = w.reshape(k, NH, per)
    w = jnp.pad(w, ((0, 0), (0, 0), (0, total - per)))
    return w.reshape(k, NH * total)


def _rope_swap_cols(w):
    k = w.shape[0]
    w = w.reshape(k, NH, 128)
    half = QK_ROPE // 2
    x1 = w[:, :, QK_NOPE:QK_NOPE + half]
    x2 = w[:, :, QK_NOPE + half:QK_NOPE + QK_ROPE]
    z = jnp.zeros_like(w)
    z = z.at[:, :, QK_NOPE:QK_NOPE + half].set(-x2)
    z = z.at[:, :, QK_NOPE + half:QK_NOPE + QK_ROPE].set(x1)
    return z.reshape(k, NH * 128)


def _pick(total, pref):
    t = min(total, pref)
    while total % t:
        t //= 2
    return t


def kernel(x, positions, mix_norm, w_in, gate_bias, conv_w, conv_out, rwkv_mu, rwkv_w0, rwkv_w_up, rwkv_a0, rwkv_a_up, rwkv_g_up, rwkv_k_k, rwkv_k_a, rwkv_r_k, rwkv_ln_g, rwkv_ln_b, rwkv_out, mla_q_norm, mla_w_uq, mla_kv_norm, mla_w_ukv, mla_out, fnet_out, w_o, ffn_norm, ffn_w_gu, ffn_w_down, final_norm):
    bsz, s_len, d = x.shape
    n = bsz * s_len
    depth = w_in.shape[0]
    n1 = s_len // FFT_N2

    inv_freq = ROPE_THETA ** (-jnp.arange(0, QK_ROPE, 2, dtype=F32) / QK_ROPE)
    ang = (positions.astype(F32)[..., None] * inv_freq).reshape(-1, 128)
    half = (bsz, s_len, QK_ROPE // 2)
    cos, sin = jnp.cos(ang).reshape(half), jnp.sin(ang).reshape(half)
    ones = jnp.ones((bsz, s_len, QK_NOPE), F32)
    zpad = jnp.zeros((bsz, s_len, 128 - QK_NOPE - QK_ROPE), F32)
    cs_tab = jnp.concatenate([ones, cos, cos, zpad], axis=-1).reshape(n, 128)
    sn_tab = jnp.concatenate([0 * ones, sin, sin, zpad], axis=-1).reshape(n, 128)

    c1, s1n, ftab, gtab = _fourier_constants(s_len)
    head_ones_np = np.kron(np.eye(NH), np.ones((HD, HD)))
    head_avg = jnp.asarray(head_ones_np / HD, BF16)
    head_mask = jnp.asarray(head_ones_np, BF16)

    place = np.zeros((128, NH * 128), np.float32)
    for h in range(NH):
        for jj in range(QK_ROPE):
            place[jj, h * 128 + QK_NOPE + jj] = 1.0
    place = jnp.asarray(place)
    place_sw = _rope_swap_cols(place)

    tm = _pick(n, 1024)
    ts = _pick(s_len, 512)
    tq = _pick(s_len, 1024)
    cuts = np.cumsum([768, 768, 384, 256, 160, 256])

    zcols = lambda k: jnp.zeros((depth, d, k), BF16)
    wb16 = w_in.astype(BF16)
    w_small = jnp.concatenate(
        [wb16[:, :, :cuts[2]], zcols(P_Q - P_LORA - 384), wb16[:, :, cuts[2]:cuts[4]],
         zcols(256 - 160), wb16[:, :, cuts[4]:cuts[5]], zcols(P_W - P_F - 256)], axis=2)
    w_gate = wb16[:, :, cuts[5]:]
    wa_all, wb_all, wc_all, wd_all, wo_all = (
        t.astype(BF16) for t in (conv_out, rwkv_out, mla_out, fnet_out, w_o))
    wgu_all = ffn_w_gu.astype(BF16)
    wdn_all = ffn_w_down.astype(BF16)

    xf = x.reshape(n, d)
    for l in range(depth):
        g_mix = mix_norm[l].reshape(1, d)
        p2 = _norm_mm(xf, g_mix, w_small, l, tm=_pick(n, 512))
        p3 = p2.reshape(bsz, s_len, P_W)

        *scan_ops, gc, bonus, rg = _rwkv_prep(
            p3, rwkv_mu[l].reshape(2, 3 * BW), rwkv_w0[l], rwkv_w_up[l], rwkv_a0[l],
            rwkv_a_up[l], rwkv_g_up[l], rwkv_k_k[l].reshape(1, BW), rwkv_k_a[l].reshape(1, BW),
            rwkv_r_k[l].reshape(1, BW), head_mask, ts=ts)
        yf, yb = _rwkv_scan(scan_ops, gc, head_mask, tb=ts)

        wq = _head_cols(mla_w_uq[l], (QK_NOPE, QK_ROPE))
        wkv = mla_w_ukv[l].reshape(KV_LORA, NH, 2 * HD)
        wk = _head_cols(wkv[:, :, :HD].reshape(KV_LORA, NH * HD), (HD,))
        wv = _head_cols(wkv[:, :, HD:].reshape(KV_LORA, NH * HD), (HD,))
        q, k, v = _mla_prep(p2, cs_tab, sn_tab, mla_q_norm[l].reshape(1, Q_LORA),
                            wq.astype(BF16), _rope_swap_cols(wq).astype(BF16),
                            mla_kv_norm[l].reshape(1, KV_LORA), wk.astype(BF16),
                            wv.astype(BF16), place, place_sw, ts=ts)
        hw = NH * 128
        mla_o = _attention(q.reshape(bsz, s_len, hw), k.reshape(bsz, s_len, hw),
                           v.reshape(bsz, s_len, hw), tq=tq, tk=_pick(s_len, 4096))

        ar, ai = _fft1(p2.reshape(bsz, n1, FFT_N2, P_W), c1, s1n, mb=16)
        fn = _fft2(ar, ai, ftab, gtab, kb=_pick(n1, 8)).reshape(bsz, s_len, BW)

        x3 = _combine(xf.reshape(bsz, s_len, d), p3, yf, yb, bonus, rg, mla_o, fn,
                      g_mix, w_gate, gate_bias[l].reshape(1, N_BRANCH * d), conv_w[l],
                      rwkv_ln_g[l].reshape(1, BW), rwkv_ln_b[l].reshape(1, BW), head_avg,
                      wa_all, wb_all, wc_all, wd_all, wo_all, l, ts=ts)
        xf = _ffn(x3.reshape(n, d), ffn_norm[l].reshape(1, d), wgu_all, wdn_all,
                  final_norm.reshape(1, d), l, tm=_pick(n, 512), final=(l == depth - 1))
    return xf.reshape(bsz, s_len, d)
```

```python
import functools

import numpy as np
import jax
import jax.numpy as jnp
from jax import lax
from jax.experimental import pallas as pl
from jax.experimental.pallas import tpu as pltpu

F32 = jnp.float32
BF16 = jnp.bfloat16

D_MODEL = 1024
N_BRANCH = 4
BW = 256
HD = 64
NH = BW // HD
LORA = 64
GATE_LORA = 128
Q_LORA = 256
KV_LORA = 128
QK_NOPE = 64
QK_ROPE = 32
D_FF = 2816
NORM_EPS = 1e-6
RWKV_LN_EPS = 64e-5
ROPE_THETA = 10000.0
CHUNK = 64
SCAN_UNROLL = 2
FFT_N2 = 128

P_CONV = 0
P_RKV = 768
P_LORA = 1536
P_Q = 2048
P_KV = 2304
P_F = 2560
P_W = 2816

VMEM_LIMIT = 56 * 1024 * 1024


def _cparams(sem):
    return pltpu.CompilerParams(dimension_semantics=sem, vmem_limit_bytes=VMEM_LIMIT)


def _resident(shape):
    return pl.BlockSpec(shape, lambda *_: (0,) * len(shape), pipeline_mode=pl.Buffered(1))


def _layer(shape, l):
    return pl.BlockSpec((None,) + shape, lambda *_: (l,) + (0,) * len(shape),
                        pipeline_mode=pl.Buffered(1))


def _dot(a, b, prec=None):
    return jnp.dot(a, b, preferred_element_type=F32, precision=prec)


def _dot_nt(a, b, prec=None):
    return lax.dot_general(a, b, (((1,), (1,)), ((), ())),
                           preferred_element_type=F32, precision=prec)


def _dot_tn(a, b, prec=None):
    return lax.dot_general(a, b, (((0,), (0,)), ((), ())),
                           preferred_element_type=F32, precision=prec)


def _dotb(a, b):
    return _dot(a.astype(BF16), b.astype(BF16))


def _bf16_terms(x, terms):
    parts = []
    for _ in range(terms):
        p = x.astype(BF16)
        parts.append(p)
        x = x - p.astype(F32)
    return parts


def _dot_exact_rhs(x, m01, terms):
    return sum(_dot(p, m01) for p in _bf16_terms(x, terms))


def _rms(xf, g, eps=NORM_EPS):
    return xf * lax.rsqrt(jnp.mean(xf * xf, axis=-1, keepdims=True) + eps) * g


def _sigmoid(z):
    return 0.5 + 0.5 * jnp.tanh(0.5 * z)


def _norm_mm_kernel(x_ref, g_ref, w_ref, o_ref):
    h = _rms(x_ref[...], g_ref[...]).astype(BF16)
    o_ref[...] = _dot(h, w_ref[...])


def _norm_mm(x, g, w, l, *, tm):
    n, d = x.shape
    nc = w.shape[2]
    return pl.pallas_call(
        _norm_mm_kernel,
        grid=(n // tm,),
        in_specs=[pl.BlockSpec((tm, d), lambda i: (i, 0)), _resident((1, d)),
                  _layer((d, nc), l)],
        out_specs=pl.BlockSpec((tm, nc), lambda i: (i, 0)),
        out_shape=jax.ShapeDtypeStruct((n, nc), F32),
        compiler_params=_cparams(("parallel",)),
        name="norm_mm",
    )(x, g, w)


def _shifted(t, prev_blk, next_blk, i, n_i):
    ts = t.shape[0]
    prev_row = jnp.where(i == 0, 0.0, prev_blk[7:8, :])
    next_row = jnp.where(i == n_i - 1, 0.0, next_blk[0:1, :])
    rows = lax.broadcasted_iota(jnp.int32, (ts, 1), 0)
    t_prev = jnp.where(rows == 0, prev_row, pltpu.roll(t, 1, axis=0))
    t_next = jnp.where(rows == ts - 1, next_row, pltpu.roll(t, ts - 1, axis=0))
    return t_prev, t_next


def _halo_specs(ts, width, col_blk, s_len):
    r8 = ts // 8
    last8 = s_len // 8 - 1
    main = pl.BlockSpec((None, ts, width), lambda b, i: (b, i, col_blk))
    prev = pl.BlockSpec((None, 8, width),
                        lambda b, i: (b, jnp.maximum(i * r8 - 1, 0), col_blk))
    nxt = pl.BlockSpec((None, 8, width),
                       lambda b, i: (b, jnp.minimum((i + 1) * r8, last8), col_blk))
    return main, prev, nxt


def _softplus(z):
    return jnp.maximum(z, 0.0) + jnp.log(1.0 + jnp.exp(-jnp.abs(z)))


def _rwkv_prep_kernel(rkv_ref, prev_ref, next_ref, lora_ref, mu_ref, w0_ref, wup_ref,
                      a0_ref, aup_ref, gup_ref, kk_ref, ka_ref, rk_ref, bd_ref,
                      at_out, rt_out, bt_out, kt_out, bh_out, kh_out, v_out, gc_out,
                      bonus_out, g_out):
    i = pl.program_id(1)
    n_i = pl.num_programs(1)
    t = rkv_ref[...]
    ts = t.shape[0]
    nck = ts // CHUNK
    t_prev, t_next = _shifted(t, prev_ref[...], next_ref[...], i, n_i)
    lora = lora_ref[...]
    bd = bd_ref[...]
    bonus = None
    per_dir = []
    for d in range(2):
        sh = t_prev if d == 0 else t_next
        mixed = t + mu_ref[d:d + 1, :] * (sh - t)
        rd = mixed[:, 0:BW]
        kd = mixed[:, BW:2 * BW]
        vd = mixed[:, 2 * BW:3 * BW]
        w_l = jnp.tanh(lora[:, d * LORA:(d + 1) * LORA])
        a_l = lora[:, 2 * LORA + d * LORA:2 * LORA + (d + 1) * LORA]
        w_pre = w0_ref[d:d + 1, :] + _dotb(w_l, wup_ref[d])
        w_log = -_softplus(-w_pre) - 0.5
        lw = -jnp.exp(w_log)
        a = _sigmoid(a0_ref[d:d + 1, :] + _dotb(a_l, aup_ref[d]))
        kk = kd * kk_ref[...]
        ss = _dot_exact_rhs(kk * kk, bd, 2)
        kk = kk / jnp.maximum(jnp.sqrt(ss), 1e-12)
        kt = kd * (1.0 + (a - 1.0) * ka_ref[...])
        bo = _dot_exact_rhs(rd * kt * rk_ref[...], bd, 2) * vd
        bonus = bo if bonus is None else bonus + bo
        per_dir.append((rd, kt, vd, kk, kk * a, lw))
    bonus_out[...] = bonus
    g_out[...] = _dotb(_sigmoid(lora[:, 4 * LORA:4 * LORA + GATE_LORA]), gup_ref[...])

    rows = lax.broadcasted_iota(jnp.int32, (ts, ts), 0)
    cols = lax.broadcasted_iota(jnp.int32, (ts, ts), 1)
    tri = jnp.where((rows // CHUNK == cols // CHUNK) & (cols <= rows), 1.0, 0.0).astype(BF16)
    lw_terms = _bf16_terms(jnp.concatenate([per_dir[0][5], per_dir[1][5]], axis=1), 3)
    pre = sum(_dot(tri, p) for p in lw_terms)
    for d in range(2):
        rd, kt, vd, kk, b, lw = per_dir[d]
        tot_c = jnp.sum(lw.reshape(nck, CHUNK, BW), axis=1)
        tot = jnp.broadcast_to(tot_c[:, None, :], (nck, CHUNK, BW)).reshape(ts, BW)
        p = pre[:, d * BW:(d + 1) * BW]
        cum = p if d == 0 else tot - p + lw
        g_inv = jnp.exp(-cum)
        g_tot_c = jnp.exp(tot_c)
        g_rem = jnp.broadcast_to(g_tot_c[:, None, :], (nck, CHUNK, BW)).reshape(ts, BW) * g_inv
        at_out[d] = (-kk * jnp.exp(cum - lw)).astype(BF16)
        rt_out[d] = (rd * jnp.exp(cum)).astype(BF16)
        bt_out[d] = (b * g_inv).astype(BF16)
        kt_out[d] = (kt * g_inv).astype(BF16)
        bh_out[d] = (b * g_rem).astype(BF16)
        kh_out[d] = (kt * g_rem).astype(BF16)
        v_out[d] = vd.astype(BF16)
        gc_out[d] = g_tot_c


def _rwkv_prep(p3, mu, w0, wup, a0, aup, gup, k_k, k_a, r_k, bd, *, ts):
    bsz, s_len, _ = p3.shape
    nck = ts // CHUNK
    main, prev, nxt = _halo_specs(ts, 3 * BW, P_RKV // (3 * BW), s_len)
    lora = pl.BlockSpec((None, ts, 384), lambda b, i: (b, i, P_LORA // 384))
    full = lambda shape: pl.BlockSpec(shape, lambda b, i: (0,) * len(shape))
    dir_out = pl.BlockSpec((2, None, ts, BW), lambda b, i: (0, b, i, 0))
    gc_out = pl.BlockSpec((2, None, nck, BW), lambda b, i: (0, b, i, 0))
    one_out = pl.BlockSpec((None, ts, BW), lambda b, i: (b, i, 0))
    dir_shape = jax.ShapeDtypeStruct((2, bsz, s_len, BW), BF16)
    gc_shape = jax.ShapeDtypeStruct((2, bsz, s_len // CHUNK, BW), F32)
    one_shape = jax.ShapeDtypeStruct((bsz, s_len, BW), F32)
    return pl.pallas_call(
        _rwkv_prep_kernel,
        grid=(bsz, s_len // ts),
        in_specs=[main, prev, nxt, lora,
                  full((2, 3 * BW)), full((2, BW)), full((2, LORA, BW)),
                  full((2, BW)), full((2, LORA, BW)), full((GATE_LORA, BW)),
                  full((1, BW)), full((1, BW)), full((1, BW)), full((BW, BW))],
        out_specs=[dir_out] * 7 + [gc_out] + [one_out] * 2,
        out_shape=[dir_shape] * 7 + [gc_shape] + [one_shape] * 2,
        compiler_params=_cparams(("parallel", "parallel")),
        name="rwkv_prep",
    )(p3, p3, p3, p3, mu, w0, wup, a0, aup, gup, k_k, k_a, r_k, bd)


def _rwkv_scan_kernel(*refs, n_chunk, bsz):
    n_in = 8
    fwd = refs[0:n_in]
    bwd = refs[n_in:2 * n_in]
    bdm_ref = refs[2 * n_in]
    yf_ref, yb_ref, h_ref = refs[2 * n_in + 1:]

    @pl.when(pl.program_id(0) == 0)
    def _():
        h_ref[...] = jnp.zeros_like(h_ref)

    bdm = bdm_ref[...]
    rows = lax.broadcasted_iota(jnp.int32, (CHUNK, BW), 0)
    cols = lax.broadcasted_iota(jnp.int32, (CHUNK, BW), 1) % CHUNK
    eye = jnp.where(rows == cols, 1.0, 0.0).astype(F32)

    def bd(x):
        xb = x.astype(BF16)
        return jnp.concatenate([xb] * NH, axis=0) * bdm

    def chunk_body(c2, carry):
        seqs = [(d, b) for d in range(2) for b in range(bsz)]
        nq = len(seqs)
        streams = [(par, d, b) for par in range(SCAN_UNROLL) for d, b in seqs]
        ns = len(streams)
        ld = []
        for par, d, b in streams:
            in_refs = fwd if d == 0 else bwd
            c = SCAN_UNROLL * c2 + par
            cidx = c if d == 0 else n_chunk - 1 - c
            sl = pl.ds(pl.multiple_of(cidx * CHUNK, CHUNK), CHUNK)
            ld.append([r[b, sl, :] for r in in_refs[:7]]
                      + [in_refs[7][b, pl.ds(cidx, 1), :], sl])
        strict = [(rows > cols) if d == 0 else (rows < cols) for _, d, _ in streams]
        incl = [(rows >= cols) if d == 0 else (rows <= cols) for _, d, _ in streams]
        m = [_dot_nt(jnp.concatenate([x[0], x[1]], axis=0),
                     jnp.concatenate([jnp.concatenate([x[2]] * NH, axis=0) * bdm,
                                      jnp.concatenate([x[3]] * NH, axis=0) * bdm], axis=0))
             for x in ld]
        a_ab = [jnp.where(strict[s], m[s][:CHUNK, :BW], 0.0) for s in range(ns)]
        a_ak = [jnp.where(strict[s], m[s][:CHUNK, BW:], 0.0) for s in range(ns)]
        a_rb = [jnp.where(incl[s], m[s][CHUNK:, :BW], 0.0).astype(BF16) for s in range(ns)]
        a_rk = [jnp.where(incl[s], m[s][CHUNK:, BW:], 0.0) for s in range(ns)]
        avs = [_dot(jnp.concatenate([a_ak[s], a_rk[s]], axis=0).astype(BF16), bd(ld[s][6]))
               for s in range(ns)]
        tm = [eye + a for a in a_ab]
        pw = [_dot(a.astype(BF16), bd(a)) for a in a_ab]
        for _ in range(4):
            res = [_dot(jnp.concatenate([tm[s], pw[s]], axis=0).astype(BF16), bd(pw[s]))
                   for s in range(ns)]
            tm = [tm[s] + res[s][:CHUNK] for s in range(ns)]
            pw = [r[CHUNK:] for r in res]
        tm = [(tm[s] + _dot(tm[s].astype(BF16), bd(pw[s]))).astype(BF16) for s in range(ns)]
        p = [_dot(tm[s], bd(ld[s][0])) for s in range(ns)]
        q = [_dot(tm[s], bd(avs[s][:CHUNK])) for s in range(ns)]
        g = [ld[s][1].astype(F32) + _dot(a_rb[s], bd(p[s])) for s in range(ns)]
        y0 = [avs[s][CHUNK:] + _dot(a_rb[s], bd(q[s])) for s in range(ns)]
        gpl = [jnp.concatenate([g[s], p[s]], axis=0).astype(BF16) for s in range(ns)]
        hst = [h_ref[i] for i in range(nq)]
        for par in range(SCAN_UNROLL):
            st =[par * nq + i for i in range(nq)]
            gp = [_dot_nt(gpl[s], hst[i].astype(BF16)) for i, s in enumerate(st)]
            for i, s in enumerate(st):
                _, d, b = streams[s]
                y_ref = yf_ref if d == 0 else yb_ref
                y_ref[b, ld[s][8], :] = gp[i][:CHUNK] + y0[s]
            upd = [_dot_tn(jnp.concatenate([(gp[i][CHUNK:] + q[s]).astype(BF16), ld[s][6]],
                                           axis=0),
                           jnp.concatenate([ld[s][4], ld[s][5]], axis=0))
                   for i, s in enumerate(st)]
            hst = [hst[i] * ld[s][7] + upd[i] * bdm.astype(F32) for i, s in enumerate(st)]
        for i in range(nq):
            h_ref[i] = hst[i]
        return carry

    lax.fori_loop(0, n_chunk // SCAN_UNROLL, chunk_body, 0)


def _rwkv_scan(ops, gc, bdm, *, tb):
    _, bsz, s_len, _ = ops[0].shape
    nb = s_len // tb
    nck = tb // CHUNK

    def specs(d):
        blk = (lambda i: i) if d == 0 else (lambda i: nb - 1 - i)
        big = pl.BlockSpec((None, bsz, tb, BW), lambda i: (d, 0, blk(i), 0))
        small = pl.BlockSpec((None, bsz, nck, BW), lambda i: (d, 0, blk(i), 0))
        return [big] * 7 + [small]

    out_f = pl.BlockSpec((bsz, tb, BW), lambda i: (0, i, 0))
    out_b = pl.BlockSpec((bsz, tb, BW), lambda i: (0, nb - 1 - i, 0))
    shape = jax.ShapeDtypeStruct((bsz, s_len, BW), F32)
    args = list(ops) + [gc]
    return pl.pallas_call(
        functools.partial(_rwkv_scan_kernel, n_chunk=nck, bsz=bsz),
        grid=(nb,),
        in_specs=specs(0) + specs(1) + [pl.BlockSpec((BW, BW), lambda i: (0, 0))],
        out_specs=[out_f, out_b],
        out_shape=[shape, shape],
        scratch_shapes=[pltpu.VMEM((2 * bsz, BW, BW), F32)],
        compiler_params=_cparams(("arbitrary",)),
        name="rwkv_scan",
    )(*args, *args, bdm)


def _mla_prep_kernel(qlo_ref, kvlo_ref, cs_ref, sn_ref, qn_ref, wq_ref, wqs_ref,
                     kvn_ref, wk_ref, wv_ref, pl_ref, pls_ref, q_out, k_out, v_out, *, scale):
    cs = jnp.concatenate([cs_ref[...]] * NH, axis=1)
    sn = jnp.concatenate([sn_ref[...]] * NH, axis=1)
    hq = _rms(qlo_ref[...], qn_ref[...]).astype(BF16)
    q = _dot(hq, wq_ref[...]) * cs + _dot(hq, wqs_ref[...]) * sn
    q_out[...] = (q * scale).astype(BF16)
    kv = kvlo_ref[...]
    hkv = _rms(kv[:, :KV_LORA], kvn_ref[...]).astype(BF16)
    kr = kv[:, KV_LORA:]
    k = (_dot(hkv, wk_ref[...]) + _dotb(kr, pl_ref[...])) * cs + _dotb(kr, pls_ref[...]) * sn
    k_out[...] = k.astype(BF16)
    v = _dot(hkv, wv_ref[...])
    lane = lax.broadcasted_iota(jnp.int32, v.shape, 1) % 128
    v_out[...] = jnp.where(lane == HD, 1.0, v).astype(BF16)


def _mla_prep(p2, cs, sn, qn, wq, wqs, kvn, wk, wv, plc, pls, *, ts):
    n = p2.shape[0]
    full = lambda shape: pl.BlockSpec(shape, lambda i: (0,) * len(shape))
    hw = NH * 128
    out = pl.BlockSpec((ts, hw), lambda i: (i, 0))
    shape = jax.ShapeDtypeStruct((n, hw), BF16)
    return pl.pallas_call(
        functools.partial(_mla_prep_kernel, scale=float((QK_NOPE + QK_ROPE) ** -0.5 * np.log2(np.e))),
        grid=(n // ts,),
        in_specs=[pl.BlockSpec((ts, 256), lambda i: (i, P_Q // 256)),
                  pl.BlockSpec((ts, 256), lambda i: (i, P_KV // 256)),
                  pl.BlockSpec((ts, 128), lambda i: (i, 0)),
                  pl.BlockSpec((ts, 128), lambda i: (i, 0)),
                  full((1, Q_LORA)), full((Q_LORA, hw)), full((Q_LORA, hw)),
                  full((1, KV_LORA)), full((KV_LORA, hw)), full((KV_LORA, hw)),
                  full((128, hw)), full((128, hw))],
        out_specs=[out] * 3,
        out_shape=[shape] * 3,
        compiler_params=_cparams(("parallel",)),
        name="mla_prep",
    )(p2, p2, cs, sn, qn, wq, wqs, kvn, wk, wv, plc, pls)


def _attn_kernel(q_ref, k_ref, v_ref, o_ref, m_ref, acc_ref, *, rb):
    j = pl.program_id(2)

    @pl.when(j == 0)
    def _():
        m_ref[...] = jnp.full_like(m_ref, -jnp.inf)
        acc_ref[...] = jnp.zeros_like(acc_ref)

    units = [(h, r) for h in range(NH) for r in range(q_ref.shape[0] // rb)]

    def scores(h, r):
        hs = slice(h * 128, (h + 1) * 128)
        return _dot_nt(q_ref[r * rb:(r + 1) * rb, hs], k_ref[:, hs])

    s_next = scores(*units[0])
    for idx, (h, r) in enumerate(units):
        s = s_next
        if idx + 1 < len(units):
            s_next = scores(*units[idx + 1])
        rows = slice(r * rb, (r + 1) * rb)
        m_old = m_ref[h, rows]
        m_new = jnp.maximum(m_old, jnp.max(s, axis=-1, keepdims=True))
        p = jnp.exp2(s - m_new).astype(BF16)
        alpha = jnp.exp2(m_old - m_new)
        acc_ref[h, rows] = alpha * acc_ref[h, rows] + _dot(p, v_ref[:, h * 128:(h + 1) * 128])
        m_ref[h, rows] = m_new

    @pl.when(j == pl.num_programs(2) - 1)
    def _():
        outs = []
        for h in range(NH):
            acc = acc_ref[h]
            outs.append(acc[:, :HD] / acc[:, HD:HD + 1])
        o_ref[...] = jnp.concatenate(outs, axis=1).astype(o_ref.dtype)


def _attention(q, k, v, *, tq, tk):
    bsz, s_len, hw = q.shape
    return pl.pallas_call(
        functools.partial(_attn_kernel, rb=_pick(tq, 256)),
        grid=(bsz, s_len // tq, s_len // tk),
        in_specs=[pl.BlockSpec((None, tq, hw), lambda b, i, j: (b, i, 0)),
                  pl.BlockSpec((None, tk, hw), lambda b, i, j: (b, j, 0)),
                  pl.BlockSpec((None, tk, hw), lambda b, i, j: (b, j, 0))],
        out_specs=pl.BlockSpec((None, tq, BW), lambda b, i, j: (b, i, 0)),
        out_shape=jax.ShapeDtypeStruct((bsz, s_len, BW), BF16),
        scratch_shapes=[pltpu.VMEM((NH, tq, 1), F32), pltpu.VMEM((NH, tq, 128), F32)],
        compiler_params=_cparams(("parallel", "parallel", "arbitrary")),
        name="mla_attention",
    )(q, k, v)


def _fft1_kernel(x_ref, c1_ref, s1n_ref, ar_out, ai_out, *, mb):
    c1 = c1_ref[...]
    s1n = s1n_ref[...]
    for m in range(mb):
        xm = x_ref[:, m, :].astype(BF16)
        ar_out[m] = _dot(c1, xm)
        ai_out[m] = _dot(s1n, xm)


def _fft1(p4, c1, s1n, *, mb):
    bsz, n1, n2, _ = p4.shape
    mat = pl.BlockSpec((n1, n1), lambda b, j: (0, 0))
    out = pl.BlockSpec((None, mb, n1, BW), lambda b, j: (b, j, 0, 0))
    shape = jax.ShapeDtypeStruct((bsz, n2, n1, BW), F32)
    return pl.pallas_call(
        functools.partial(_fft1_kernel, mb=mb),
        grid=(bsz, n2 // mb),
        in_specs=[pl.BlockSpec((None, n1, mb, BW), lambda b, j: (b, 0, j, P_F // BW)),
                  mat, mat],
        out_specs=[out, out],
        out_shape=[shape, shape],
        compiler_params=_cparams(("parallel", "parallel")),
        name="fft_stage1",
    )(p4, c1, s1n)


def _fft2_kernel(ar_ref, ai_ref, f_ref, g_ref, o_ref, *, kb):
    g = g_ref[...]
    for q in range(kb):
        ar = ar_ref[:, q, :]
        ai = ai_ref[:, q, :]
        rhs = jnp.concatenate([jnp.concatenate([ar, ai], axis=1),
                               jnp.concatenate([ai, -ar], axis=1)], axis=0).astype(BF16)
        uri = _dot(f_ref[q], rhs)
        o_ref[:, q, :] = _dot(uri.astype(BF16), g)


def _fft2(ar, ai, ftab, gtab, *, kb):
    bsz, n2, n1, _ = ar.shape
    blk = pl.BlockSpec((None, n2, kb, BW), lambda b, j: (b, 0, j, 0))
    return pl.pallas_call(
        functools.partial(_fft2_kernel, kb=kb),
        grid=(bsz, n1 // kb),
        in_specs=[blk, blk,
                  pl.BlockSpec((kb, n2, 2 * n2), lambda b, j: (j, 0, 0)),
                  pl.BlockSpec((2 * BW, BW), lambda b, j: (0, 0))],
        out_specs=blk,
        out_shape=jax.ShapeDtypeStruct((bsz, n2, n1, BW), F32),
        compiler_params=_cparams(("parallel", "parallel")),
        name="fft_stage2",
    )(ar, ai, ftab, gtab)


def _dft_mats(n):
    idx = np.arange(n)
    ang = 2.0 * np.pi * ((idx[:, None] * idx[None, :]) % n) / n
    return np.cos(ang), np.sin(ang)


def _fourier_constants(s_len):
    n2 = FFT_N2
    n1 = s_len // n2
    c1, s1 = _dft_mats(n1)
    k1 = np.arange(n1)[:, None, None]
    k2 = np.arange(n2)[None, :, None]
    m2 = np.arange(n2)[None, None, :]
    ang = 2.0 * np.pi * ((m2 * (n1 * k2 + k1)) % s_len) / s_len
    ftab = np.concatenate([np.cos(ang), np.sin(ang)], axis=2)
    cg64, sg64 = _dft_mats(HD)
    norm = 1.0 / np.sqrt(float(s_len) * HD)
    gtab = np.concatenate([np.kron(np.eye(NH), cg64), np.kron(np.eye(NH), sg64)], axis=0) * norm
    f = lambda a: jnp.asarray(a, BF16)
    return f(c1), f(-s1), f(ftab), f(gtab)


def _combine_kernel(x_ref, cv_ref, cvp_ref, cvn_ref, yf_ref, yb_ref, bonus_ref, rg_ref,
                    mla_ref, fn_ref, gn_ref, wg_ref, gb_ref, cw_ref, lng_ref, lnb_ref, avg_ref,
                    wa_ref, wb_ref, wc_ref, wd_ref, wo_ref, o_ref):
    i = pl.program_id(1)
    n_i = pl.num_programs(1)
    cv = cv_ref[...]
    cvp = cvp_ref[...]
    cvn = cvn_ref[...]
    z = cv[:, 2 * BW:] * cv[:, :BW]
    zp, zn = _shifted(z, cvp[:, 2 * BW:] * cvp[:, :BW], cvn[:, 2 * BW:] * cvn[:, :BW], i, n_i)
    conv = zp * cw_ref[0:1, :] + z * cw_ref[1:2, :] + zn * cw_ref[2:3, :]
    ya = _dot((cv[:, BW:2 * BW] * conv).astype(BF16), wa_ref[...])
    y = yf_ref[...] + yb_ref[...]
    avg = avg_ref[...]
    mean = _dot_exact_rhs(y, avg, 2)
    yc = y - mean
    var = _dot_exact_rhs(yc * yc, avg, 2)
    yn = yc * lax.rsqrt(var + RWKV_LN_EPS) * lng_ref[...] + lnb_ref[...]
    yb = _dot(((yn + bonus_ref[...]) * rg_ref[...]).astype(BF16), wb_ref[...])
    yc_ = _dot(mla_ref[...], wc_ref[...])
    yd = _dot(fn_ref[...].astype(BF16), wd_ref[...])
    x = x_ref[...]
    d = D_MODEL
    h = _rms(x, gn_ref[...]).astype(BF16)
    mix = None
    for br, yb_ in enumerate((ya, yb, yc_, yd)):
        cols = slice(br * d, (br + 1) * d)
        term = _sigmoid(_dot(h, wg_ref[:, cols]) + gb_ref[:, cols]) * yb_
        mix = term if mix is None else mix + term
    o_ref[...] = x + _dot(mix.astype(BF16), wo_ref[...])


def _combine(x3, p3, yf, yb, bonus, rg, mla_o, fn, gn, wg, gb, cw, lng, lnb, avg,
             wa, wb, wc, wd, wo, l, *, ts):
    bsz, s_len, d = x3.shape
    main, prev, nxt = _halo_specs(ts, 3 * BW, P_CONV // (3 * BW), s_len)
    row = lambda w: pl.BlockSpec((None, ts, w), lambda b, i: (b, i, 0))
    res = _resident
    lay = lambda shape: _layer(shape, l)
    return pl.pallas_call(
        _combine_kernel,
        grid=(bsz, s_len // ts),
        in_specs=[row(d), main, prev, nxt, row(BW), row(BW), row(BW), row(BW),
                  row(BW), row(BW),
                  res((1, d)), lay((d, N_BRANCH * d)), res((1, N_BRANCH * d)),
                  res((3, BW)), res((1, BW)), res((1, BW)), res((BW, BW)),
                  lay((BW, d)), lay((BW, d)), lay((BW, d)), lay((BW, d)),
                  lay((d, d))],
        out_specs=row(d),
        out_shape=jax.ShapeDtypeStruct(x3.shape, F32),
        compiler_params=_cparams(("parallel", "parallel")),
        name="combine",
    )(x3, p3, p3, p3, yf, yb, bonus, rg, mla_o, fn, gn, wg, gb, cw, lng, lnb, avg,
      wa, wb, wc, wd, wo)


def _ffn_kernel(x_ref, g_ref, wgu_ref, wd_ref, fg_ref, o_ref, *, final):
    x = x_ref[...]
    h = _rms(x, g_ref[...]).astype(BF16)
    gu = _dot(h, wgu_ref[...])
    gt = gu[:, :D_FF]
    act = (gt * _sigmoid(gt) * gu[:, D_FF:]).astype(BF16)
    y = x + _dot(act, wd_ref[...])
    if final:
        y = _rms(y, fg_ref[...])
    o_ref[...] = y


def _ffn(x, g, wgu, wd, fg, l, *, tm, final):
    n, d = x.shape
    return pl.pallas_call(
        functools.partial(_ffn_kernel, final=final),
        grid=(n // tm,),
        in_specs=[pl.BlockSpec((tm, d), lambda i: (i, 0)),
                  _resident((1, d)), _layer((d, 2 * D_FF), l), _layer((D_FF, d), l),
                  _resident((1, d))],
        out_specs=pl.BlockSpec((tm, d), lambda i: (i, 0)),
        out_shape=jax.ShapeDtypeStruct((n, d), F32),
        compiler_params=_cparams(("parallel",)),
        name="ffn",
    )(x, g, wgu, wd, fg)


def _head_cols(w, widths, total=128):
    k = w.shape[0]
    per = sum(widths)
    w = w.reshape(k, NH, per)
    w = jnp.pad(w, ((0, 0), (0, 0), (0, total - per)))
    return w.reshape(k, NH * total)


def _rope_swap_cols(w):
    k = w.shape[0]
    w = w.reshape(k, NH, 128)
    half = QK_ROPE // 2
    x1 = w[:, :, QK_NOPE:QK_NOPE + half]
    x2 = w[:, :, QK_NOPE + half:QK_NOPE + QK_ROPE]
    z = jnp.zeros_like(w)
    z = z.at[:, :, QK_NOPE:QK_NOPE + half].set(-x2)
    z = z.at[:, :, QK_NOPE + half:QK_NOPE + QK_ROPE].set(x1)
    return z.reshape(k, NH * 128)


def _pick(total, pref):
    t = min(total, pref)
    while total % t:
        t //= 2
    return t


def kernel(x, positions, mix_norm, w_in, gate_bias, conv_w, conv_out, rwkv_mu, rwkv_w0, rwkv_w_up, rwkv_a0, rwkv_a_up, rwkv_g_up, rwkv_k_k, rwkv_k_a, rwkv_r_k, rwkv_ln_g, rwkv_ln_b, rwkv_out, mla_q_norm, mla_w_uq, mla_kv_norm, mla_w_ukv, mla_out, fnet_out, w_o, ffn_norm, ffn_w_gu, ffn_w_down, final_norm):
    bsz, s_len, d = x.shape
    n = bsz * s_len
    depth = w_in.shape[0]
    n1 = s_len // FFT_N2

    inv_freq = ROPE_THETA ** (-jnp.arange(0, QK_ROPE, 2, dtype=F32) / QK_ROPE)
    pos8 = positions.astype(F32).reshape(bsz, s_len // 8, 8)
    ang = jnp.repeat(pos8, QK_ROPE // 2, axis=-1) * jnp.tile(inv_freq, 8)
    half = (bsz, s_len, QK_ROPE // 2)
    cos, sin = jnp.cos(ang).reshape(half), jnp.sin(ang).reshape(half)
    ones = jnp.ones((bsz, s_len, QK_NOPE), F32)
    zpad = jnp.zeros((bsz, s_len, 128 - QK_NOPE - QK_ROPE), F32)
    cs_tab = jnp.concatenate([ones, cos, cos, zpad], axis=-1).reshape(n, 128)
    sn_tab = jnp.concatenate([0 * ones, sin, sin, zpad], axis=-1).reshape(n, 128)

    c1, s1n, ftab, gtab = _fourier_constants(s_len)
    head_ones_np = np.kron(np.eye(NH), np.ones((HD, HD)))
    head_avg = jnp.asarray(head_ones_np / HD, BF16)
    head_mask = jnp.asarray(head_ones_np, BF16)

    place = np.zeros((128, NH * 128), np.float32)
    for h in range(NH):
        for jj in range(QK_ROPE):
            place[jj, h * 128 + QK_NOPE + jj] = 1.0
    place = jnp.asarray(place)
    place_sw = _rope_swap_cols(place)

    tm = _pick(n, 1024)
    ts = _pick(s_len, 512)
    tq = _pick(s_len, 1024)
    cuts = np.cumsum([768, 768, 384, 256, 160, 256])

    zcols = lambda k: jnp.zeros((depth, d, k), BF16)
    wb16 = w_in.astype(BF16)
    w_small = jnp.concatenate(
        [wb16[:, :, :cuts[2]], zcols(P_Q - P_LORA - 384), wb16[:, :, cuts[2]:cuts[4]],
         zcols(256 - 160), wb16[:, :, cuts[4]:cuts[5]], zcols(P_W - P_F - 256)], axis=2)
    w_gate = wb16[:, :, cuts[5]:]
    wa_all, wb_all, wc_all, wd_all, wo_all = (
        t.astype(BF16) for t in (conv_out, rwkv_out, mla_out, fnet_out, w_o))
    wgu_all = ffn_w_gu.astype(BF16)
    wdn_all = ffn_w_down.astype(BF16)

    xf = x.reshape(n, d)
    for l in range(depth):
        g_mix = mix_norm[l].reshape(1, d)
        p2 = _norm_mm(xf, g_mix, w_small, l, tm=_pick(n, 512))
        p3 = p2.reshape(bsz, s_len, P_W)

        *scan_ops, gc, bonus, rg = _rwkv_prep(
            p3, rwkv_mu[l].reshape(2, 3 * BW), rwkv_w0[l], rwkv_w_up[l], rwkv_a0[l],
            rwkv_a_up[l], rwkv_g_up[l], rwkv_k_k[l].reshape(1, BW), rwkv_k_a[l].reshape(1, BW),
            rwkv_r_k[l].reshape(1, BW), head_mask, ts=ts)
        yf, yb = _rwkv_scan(scan_ops, gc, head_mask, tb=ts)

        wq = _head_cols(mla_w_uq[l], (QK_NOPE, QK_ROPE))
        wkv = mla_w_ukv[l].reshape(KV_LORA, NH, 2 * HD)
        wk = _head_cols(wkv[:, :, :HD].reshape(KV_LORA, NH * HD), (HD,))
        wv = _head_cols(wkv[:, :, HD:].reshape(KV_LORA, NH * HD), (HD,))
        q, k, v = _mla_prep(p2, cs_tab, sn_tab, mla_q_norm[l].reshape(1, Q_LORA),
                            wq.astype(BF16), _rope_swap_cols(wq).astype(BF16),
                            mla_kv_norm[l].reshape(1, KV_LORA), wk.astype(BF16),
                            wv.astype(BF16), place, place_sw, ts=ts)
        hw = NH * 128
        mla_o = _attention(q.reshape(bsz, s_len, hw), k.reshape(bsz, s_len, hw),
                           v.reshape(bsz, s_len, hw), tq=tq, tk=_pick(s_len, 4096))

        ar, ai = _fft1(p2.reshape(bsz, n1, FFT_N2, P_W), c1, s1n, mb=16)
        fn = _fft2(ar, ai, ftab, gtab, kb=_pick(n1, 8)).reshape(bsz, s_len, BW)

        x3 = _combine(xf.reshape(bsz, s_len, d), p3, yf, yb, bonus, rg, mla_o, fn,
                      g_mix, w_gate, gate_bias[l].reshape(1, N_BRANCH * d), conv_w[l],
                      rwkv_ln_g[l].reshape(1, BW), rwkv_ln_b[l].reshape(1, BW), head_avg,
                      wa_all, wb_all, wc_all, wd_all, wo_all, l, ts=ts)
        xf = _ffn(x3.reshape(n, d), ffn_norm[l].reshape(1, d), wgu_all, wdn_all,
                  final_norm.reshape(1, d), l, tm=_pick(n, 512), final=(l == depth - 1))
    return xf.reshape(bsz, s_len, d)
```

```python
import functools

import numpy as np
import jax
import jax.numpy as jnp
from jax import lax
from jax.experimental import pallas as pl
from jax.experimental.pallas import tpu as pltpu

F32 = jnp.float32
BF16 = jnp.bfloat16

D_MODEL = 1024
N_BRANCH = 4
BW = 256
HD = 64
NH = BW // HD
LORA = 64
GATE_LORA = 128
Q_LORA = 256
KV_LORA = 128
QK_NOPE = 64
QK_ROPE = 32
D_FF = 2816
NORM_EPS = 1e-6
RWKV_LN_EPS = 64e-5
ROPE_THETA = 10000.0
CHUNK = 64
SCAN_UNROLL = 2
FFT_N2 = 128

P_CONV = 0
P_RKV = 768
P_LORA = 1536
P_Q = 2048
P_KV = 2304
P_F = 2560
P_W = 2816

VMEM_LIMIT = 56 * 1024 * 1024


def _cparams(sem):
    return pltpu.CompilerParams(dimension_semantics=sem, vmem_limit_bytes=VMEM_LIMIT)


def _resident(shape):
    return pl.BlockSpec(shape, lambda *_: (0,) * len(shape), pipeline_mode=pl.Buffered(1))


def _layer(shape, l):
    return pl.BlockSpec((None,) + shape, lambda *_: (l,) + (0,) * len(shape),
                        pipeline_mode=pl.Buffered(1))


def _dot(a, b):
    return jnp.dot(a, b, preferred_element_type=F32)


def _dot_nt(a, b):
    return lax.dot_general(a, b, (((1,), (1,)), ((), ())), preferred_element_type=F32)


def _dot_tn(a, b):
    return lax.dot_general(a, b, (((0,), (0,)), ((), ())), preferred_element_type=F32)


def _dotb(a, b):
    return _dot(a.astype(BF16), b.astype(BF16))


def _bf16_terms(x, terms):
    parts = []
    for _ in range(terms):
        p = x.astype(BF16)
        parts.append(p)
        x = x - p.astype(F32)
    return parts


def _dot_exact_rhs(x, m01, terms):
    return sum(_dot(p, m01) for p in _bf16_terms(x, terms))


def _rms(xf, g, eps=NORM_EPS):
    return xf * lax.rsqrt(jnp.mean(xf * xf, axis=-1, keepdims=True) + eps) * g


def _sigmoid(z):
    return 0.5 + 0.5 * jnp.tanh(0.5 * z)


def _norm_mm_kernel(x_ref, g_ref, w_ref, o_ref):
    h = _rms(x_ref[...], g_ref[...]).astype(BF16)
    o_ref[...] = _dot(h, w_ref[...])


def _norm_mm(x, g, w, l, *, tm):
    n, d = x.shape
    nc = w.shape[2]
    return pl.pallas_call(
        _norm_mm_kernel,
        grid=(n // tm,),
        in_specs=[pl.BlockSpec((tm, d), lambda i: (i, 0)), _resident((1, d)),
                  _layer((d, nc), l)],
        out_specs=pl.BlockSpec((tm, nc), lambda i: (i, 0)),
        out_shape=jax.ShapeDtypeStruct((n, nc), F32),
        compiler_params=_cparams(("parallel",)),
        name="norm_mm",
    )(x, g, w)


def _shifted(t, prev_blk, next_blk, i, n_i):
    ts = t.shape[0]
    prev_row = jnp.where(i == 0, 0.0, prev_blk[7:8, :])
    next_row = jnp.where(i == n_i - 1, 0.0, next_blk[0:1, :])
    rows = lax.broadcasted_iota(jnp.int32, (ts, 1), 0)
    t_prev = jnp.where(rows == 0, prev_row, pltpu.roll(t, 1, axis=0))
    t_next = jnp.where(rows == ts - 1, next_row, pltpu.roll(t, ts - 1, axis=0))
    return t_prev, t_next


def _halo_specs(ts, width, col_blk, s_len):
    r8 = ts // 8
    last8 = s_len // 8 - 1
    main = pl.BlockSpec((None, ts, width), lambda b, i: (b, i, col_blk))
    prev = pl.BlockSpec((None, 8, width),
                        lambda b, i: (b, jnp.maximum(i * r8 - 1, 0), col_blk))
    nxt = pl.BlockSpec((None, 8, width),
                       lambda b, i: (b, jnp.minimum((i + 1) * r8, last8), col_blk))
    return main, prev, nxt


def _softplus(z):
    return jnp.maximum(z, 0.0) + jnp.log(1.0 + jnp.exp(-jnp.abs(z)))


def _rwkv_prep_kernel(rkv_ref, prev_ref, next_ref, lora_ref, mu_ref, w0_ref, wup_ref,
                      a0_ref, aup_ref, gup_ref, kk_ref, ka_ref, rk_ref, bd_ref,
                      at_out, rt_out, bt_out, kt_out, bh_out, kh_out, v_out, gc_out,
                      bonus_out, g_out):
    i = pl.program_id(1)
    n_i = pl.num_programs(1)
    t = rkv_ref[...]
    ts = t.shape[0]
    nck = ts // CHUNK
    t_prev, t_next = _shifted(t, prev_ref[...], next_ref[...], i, n_i)
    lora = lora_ref[...]
    bd = bd_ref[...]
    bonus = None
    per_dir = []
    for d in range(2):
        sh = t_prev if d == 0 else t_next
        mixed = t + mu_ref[d:d + 1, :] * (sh - t)
        rd = mixed[:, 0:BW]
        kd = mixed[:, BW:2 * BW]
        vd = mixed[:, 2 * BW:3 * BW]
        w_l = jnp.tanh(lora[:, d * LORA:(d + 1) * LORA])
        a_l = lora[:, 2 * LORA + d * LORA:2 * LORA + (d + 1) * LORA]
        w_pre = w0_ref[d:d + 1, :] + _dotb(w_l, wup_ref[d])
        w_log = -_softplus(-w_pre) - 0.5
        lw = -jnp.exp(w_log)
        a = _sigmoid(a0_ref[d:d + 1, :] + _dotb(a_l, aup_ref[d]))
        kk = kd * kk_ref[...]
        ss = _dot_exact_rhs(kk * kk, bd, 2)
        kk = kk / jnp.maximum(jnp.sqrt(ss), 1e-12)
        kt = kd * (1.0 + (a - 1.0) * ka_ref[...])
        bo = _dot_exact_rhs(rd * kt * rk_ref[...], bd, 2) * vd
        bonus = bo if bonus is None else bonus + bo
        per_dir.append((rd, kt, vd, kk, kk * a, lw))
    bonus_out[...] = bonus
    g_out[...] = _dotb(_sigmoid(lora[:, 4 * LORA:4 * LORA + GATE_LORA]), gup_ref[...])

    rows = lax.broadcasted_iota(jnp.int32, (ts, ts), 0)
    cols = lax.broadcasted_iota(jnp.int32, (ts, ts), 1)
    tri = jnp.where((rows // CHUNK == cols // CHUNK) & (cols <= rows), 1.0, 0.0).astype(BF16)
    lw_terms = _bf16_terms(jnp.concatenate([per_dir[0][5], per_dir[1][5]], axis=1), 3)
    pre = sum(_dot(tri, p) for p in lw_terms)
    for d in range(2):
        rd, kt, vd, kk, b, lw = per_dir[d]
        tot_c = jnp.sum(lw.reshape(nck, CHUNK, BW), axis=1)
        tot = jnp.broadcast_to(tot_c[:, None, :], (nck, CHUNK, BW)).reshape(ts, BW)
        p = pre[:, d * BW:(d + 1) * BW]
        cum = p if d == 0 else tot - p + lw
        g_inv = jnp.exp(-cum)
        g_tot_c = jnp.exp(tot_c)
        g_rem = jnp.broadcast_to(g_tot_c[:, None, :], (nck, CHUNK, BW)).reshape(ts, BW) * g_inv
        at_out[d] = (-kk * jnp.exp(cum - lw)).astype(BF16)
        rt_out[d] = (rd * jnp.exp(cum)).astype(BF16)
        bt_out[d] = (b * g_inv).astype(BF16)
        kt_out[d] = (kt * g_inv).astype(BF16)
        bh_out[d] = (b * g_rem).astype(BF16)
        kh_out[d] = (kt * g_rem).astype(BF16)
        v_out[d] = vd.astype(BF16)
        gc_out[d] = g_tot_c


def _rwkv_prep(p3, mu, w0, wup, a0, aup, gup, k_k, k_a, r_k, bd, *, ts):
    bsz, s_len, _ = p3.shape
    nck = ts // CHUNK
    main, prev, nxt = _halo_specs(ts, 3 * BW, P_RKV // (3 * BW), s_len)
    lora = pl.BlockSpec((None, ts, 384), lambda b, i: (b, i, P_LORA // 384))
    full = lambda shape: pl.BlockSpec(shape, lambda b, i: (0,) * len(shape))
    dir_out = pl.BlockSpec((2, None, ts, BW), lambda b, i: (0, b, i, 0))
    gc_out = pl.BlockSpec((2, None, nck, BW), lambda b, i: (0, b, i, 0))
    one_out = pl.BlockSpec((None, ts, BW), lambda b, i: (b, i, 0))
    dir_shape = jax.ShapeDtypeStruct((2, bsz, s_len, BW), BF16)
    gc_shape = jax.ShapeDtypeStruct((2, bsz, s_len // CHUNK, BW), F32)
    one_shape = jax.ShapeDtypeStruct((bsz, s_len, BW), F32)
    return pl.pallas_call(
        _rwkv_prep_kernel,
        grid=(bsz, s_len // ts),
        in_specs=[main, prev, nxt, lora,
                  full((2, 3 * BW)), full((2, BW)), full((2, LORA, BW)),
                  full((2, BW)), full((2, LORA, BW)), full((GATE_LORA, BW)),
                  full((1, BW)), full((1, BW)), full((1, BW)), full((BW, BW))],
        out_specs=[dir_out] * 7 + [gc_out] + [one_out] * 2,
        out_shape=[dir_shape] * 7 + [gc_shape] + [one_shape] * 2,
        compiler_params=_cparams(("parallel", "parallel")),
        name="rwkv_prep",
    )(p3, p3, p3, p3, mu, w0, wup, a0, aup, gup, k_k, k_a, r_k, bd)


def _rwkv_scan_kernel(*refs, n_chunk, bsz):
    n_in = 8
    fwd = refs[0:n_in]
    bwd = refs[n_in:2 * n_in]
    bdm_ref = refs[2 * n_in]
    yf_ref, yb_ref, h_ref = refs[2 * n_in + 1:]

    @pl.when(pl.program_id(0) == 0)
    def _():
        h_ref[...] = jnp.zeros_like(h_ref)

    bdm = bdm_ref[...]
    rows = lax.broadcasted_iota(jnp.int32, (CHUNK, BW), 0)
    cols = lax.broadcasted_iota(jnp.int32, (CHUNK, BW), 1) % CHUNK
    eye = jnp.where(rows == cols, 1.0, 0.0).astype(F32)

    def bd(x):
        xb = x.astype(BF16)
        return jnp.concatenate([xb] * NH, axis=0) * bdm

    def chunk_body(c2, carry):
        seqs = [(d, b) for d in range(2) for b in range(bsz)]
        nq = len(seqs)
        streams = [(par, d, b) for par in range(SCAN_UNROLL) for d, b in seqs]
        ns = len(streams)
        ld = []
        for par, d, b in streams:
            in_refs = fwd if d == 0 else bwd
            c = SCAN_UNROLL * c2 + par
            cidx = c if d == 0 else n_chunk - 1 - c
            sl = pl.ds(pl.multiple_of(cidx * CHUNK, CHUNK), CHUNK)
            ld.append([r[b, sl, :] for r in in_refs[:7]]
                      + [in_refs[7][b, pl.ds(cidx, 1), :], sl])
        strict = [(rows > cols) if d == 0 else (rows < cols) for _, d, _ in streams]
        incl = [(rows >= cols) if d == 0 else (rows <= cols) for _, d, _ in streams]
        m = [_dot_nt(jnp.concatenate([x[0], x[1]], axis=0),
                     jnp.concatenate([jnp.concatenate([x[2]] * NH, axis=0) * bdm,
                                      jnp.concatenate([x[3]] * NH, axis=0) * bdm], axis=0))
             for x in ld]
        a_ab = [jnp.where(strict[s], m[s][:CHUNK, :BW], 0.0) for s in range(ns)]
        a_ak = [jnp.where(strict[s], m[s][:CHUNK, BW:], 0.0) for s in range(ns)]
        a_rb = [jnp.where(incl[s], m[s][CHUNK:, :BW], 0.0).astype(BF16) for s in range(ns)]
        a_rk = [jnp.where(incl[s], m[s][CHUNK:, BW:], 0.0) for s in range(ns)]
        avs = [_dot(jnp.concatenate([a_ak[s], a_rk[s]], axis=0).astype(BF16), bd(ld[s][6]))
               for s in range(ns)]
        tm = [eye + a for a in a_ab]
        pw = [_dot(a.astype(BF16), bd(a)) for a in a_ab]
        for _ in range(4):
            res = [_dot(jnp.concatenate([tm[s], pw[s]], axis=0).astype(BF16), bd(pw[s]))
                   for s in range(ns)]
            tm = [tm[s] + res[s][:CHUNK] for s in range(ns)]
            pw = [r[CHUNK:] for r in res]
        tm = [(tm[s] + _dot(tm[s].astype(BF16), bd(pw[s]))).astype(BF16) for s in range(ns)]
        p = [_dot(tm[s], bd(ld[s][0])) for s in range(ns)]
        q = [_dot(tm[s], bd(avs[s][:CHUNK])) for s in range(ns)]
        g = [ld[s][1].astype(F32) + _dot(a_rb[s], bd(p[s])) for s in range(ns)]
        y0 = [avs[s][CHUNK:] + _dot(a_rb[s], bd(q[s])) for s in range(ns)]
        gpl = [jnp.concatenate([g[s], p[s]], axis=0).astype(BF16) for s in range(ns)]
        hst = [h_ref[i] for i in range(nq)]
        for par in range(SCAN_UNROLL):
            st =[par * nq + i for i in range(nq)]
            gp = [_dot_nt(gpl[s], hst[i].astype(BF16)) for i, s in enumerate(st)]
            for i, s in enumerate(st):
                _, d, b = streams[s]
                y_ref = yf_ref if d == 0 else yb_ref
                y_ref[b, ld[s][8], :] = gp[i][:CHUNK] + y0[s]
            upd = [_dot_tn(jnp.concatenate([(gp[i][CHUNK:] + q[s]).astype(BF16), ld[s][6]],
                                           axis=0),
                           jnp.concatenate([ld[s][4], ld[s][5]], axis=0))
                   for i, s in enumerate(st)]
            hst = [hst[i] * ld[s][7] + upd[i] * bdm.astype(F32) for i, s in enumerate(st)]
        for i in range(nq):
            h_ref[i] = hst[i]
        return carry

    lax.fori_loop(0, n_chunk // SCAN_UNROLL, chunk_body, 0)


def _rwkv_scan(ops, gc, bdm, *, tb):
    _, bsz, s_len, _ = ops[0].shape
    nb = s_len // tb
    nck = tb // CHUNK

    def specs(d):
        blk = (lambda i: i) if d == 0 else (lambda i: nb - 1 - i)
        big = pl.BlockSpec((None, bsz, tb, BW), lambda i: (d, 0, blk(i), 0))
        small = pl.BlockSpec((None, bsz, nck, BW), lambda i: (d, 0, blk(i), 0))
        return [big] * 7 + [small]

    out_f = pl.BlockSpec((bsz, tb, BW), lambda i: (0, i, 0))
    out_b = pl.BlockSpec((bsz, tb, BW), lambda i: (0, nb - 1 - i, 0))
    shape = jax.ShapeDtypeStruct((bsz, s_len, BW), F32)
    args = list(ops) + [gc]
    return pl.pallas_call(
        functools.partial(_rwkv_scan_kernel, n_chunk=nck, bsz=bsz),
        grid=(nb,),
        in_specs=specs(0) + specs(1) + [pl.BlockSpec((BW, BW), lambda i: (0, 0))],
        out_specs=[out_f, out_b],
        out_shape=[shape, shape],
        scratch_shapes=[pltpu.VMEM((2 * bsz, BW, BW), F32)],
        compiler_params=_cparams(("arbitrary",)),
        name="rwkv_scan",
    )(*args, *args, bdm)


def _mla_prep_kernel(qlo_ref, kvlo_ref, cs_ref, sn_ref, qn_ref, wq_ref, wqs_ref,
                     kvn_ref, wk_ref, wv_ref, pl_ref, pls_ref, q_out, k_out, v_out, *, scale):
    cs = jnp.concatenate([cs_ref[...]] * NH, axis=1)
    sn = jnp.concatenate([sn_ref[...]] * NH, axis=1)
    hq = _rms(qlo_ref[...], qn_ref[...]).astype(BF16)
    q = _dot(hq, wq_ref[...]) * cs + _dot(hq, wqs_ref[...]) * sn
    q_out[...] = (q * scale).astype(BF16)
    kv = kvlo_ref[...]
    hkv = _rms(kv[:, :KV_LORA], kvn_ref[...]).astype(BF16)
    kr = kv[:, KV_LORA:]
    k = (_dot(hkv, wk_ref[...]) + _dotb(kr, pl_ref[...])) * cs + _dotb(kr, pls_ref[...]) * sn
    k_out[...] = k.astype(BF16)
    v = _dot(hkv, wv_ref[...])
    lane = lax.broadcasted_iota(jnp.int32, v.shape, 1) % 128
    v_out[...] = jnp.where(lane == HD, 1.0, v).astype(BF16)


def _mla_prep(p2, cs, sn, qn, wq, wqs, kvn, wk, wv, plc, pls, *, ts):
    n = p2.shape[0]
    full = lambda shape: pl.BlockSpec(shape, lambda i: (0,) * len(shape))
    hw = NH * 128
    out = pl.BlockSpec((ts, hw), lambda i: (i, 0))
    shape = jax.ShapeDtypeStruct((n, hw), BF16)
    return pl.pallas_call(
        functools.partial(_mla_prep_kernel, scale=float((QK_NOPE + QK_ROPE) ** -0.5 * np.log2(np.e))),
        grid=(n // ts,),
        in_specs=[pl.BlockSpec((ts, 256), lambda i: (i, P_Q // 256)),
                  pl.BlockSpec((ts, 256), lambda i: (i, P_KV // 256)),
                  pl.BlockSpec((ts, 128), lambda i: (i, 0)),
                  pl.BlockSpec((ts, 128), lambda i: (i, 0)),
                  full((1, Q_LORA)), full((Q_LORA, hw)), full((Q_LORA, hw)),
                  full((1, KV_LORA)), full((KV_LORA, hw)), full((KV_LORA, hw)),
                  full((128, hw)), full((128, hw))],
        out_specs=[out] * 3,
        out_shape=[shape] * 3,
        compiler_params=_cparams(("parallel",)),
        name="mla_prep",
    )(p2, p2, cs, sn, qn, wq, wqs, kvn, wk, wv, plc, pls)


def _attn_kernel(q_ref, k_ref, v_ref, o_ref, m_ref, acc_ref, *, rb):
    j = pl.program_id(2)

    @pl.when(j == 0)
    def _():
        m_ref[...] = jnp.full_like(m_ref, -jnp.inf)
        acc_ref[...] = jnp.zeros_like(acc_ref)

    units = [(h, r) for h in range(NH) for r in range(q_ref.shape[0] // rb)]

    def scores(h, r):
        hs = slice(h * 128, (h + 1) * 128)
        return _dot_nt(q_ref[r * rb:(r + 1) * rb, hs], k_ref[:, hs])

    s_next = scores(*units[0])
    for idx, (h, r) in enumerate(units):
        s = s_next
        if idx + 1 < len(units):
            s_next = scores(*units[idx + 1])
        rows = slice(r * rb, (r + 1) * rb)
        m_old = m_ref[h, rows]
        m_new = jnp.maximum(m_old, jnp.max(s, axis=-1, keepdims=True))
        p = jnp.exp2(s - m_new).astype(BF16)
        alpha = jnp.exp2(m_old - m_new)
        acc_ref[h, rows] = alpha * acc_ref[h, rows] + _dot(p, v_ref[:, h * 128:(h + 1) * 128])
        m_ref[h, rows] = m_new

    @pl.when(j == pl.num_programs(2) - 1)
    def _():
        outs = []
        for h in range(NH):
            acc = acc_ref[h]
            outs.append(acc[:, :HD] / acc[:, HD:HD + 1])
        o_ref[...] = jnp.concatenate(outs, axis=1).astype(o_ref.dtype)


def _attention(q, k, v, *, tq, tk):
    bsz, s_len, hw = q.shape
    return pl.pallas_call(
        functools.partial(_attn_kernel, rb=_pick(tq, 256)),
        grid=(bsz, s_len // tq, s_len // tk),
        in_specs=[pl.BlockSpec((None, tq, hw), lambda b, i, j: (b, i, 0)),
                  pl.BlockSpec((None, tk, hw), lambda b, i, j: (b, j, 0)),
                  pl.BlockSpec((None, tk, hw), lambda b, i, j: (b, j, 0))],
        out_specs=pl.BlockSpec((None, tq, BW), lambda b, i, j: (b, i, 0)),
        out_shape=jax.ShapeDtypeStruct((bsz, s_len, BW), BF16),
        scratch_shapes=[pltpu.VMEM((NH, tq, 1), F32), pltpu.VMEM((NH, tq, 128), F32)],
        compiler_params=_cparams(("parallel", "parallel", "arbitrary")),
        name="mla_attention",
    )(q, k, v)


def _fft1_kernel(x_ref, c1_ref, s1n_ref, ar_out, ai_out, *, mb):
    c1 = c1_ref[...].astype(BF16)
    s1n = s1n_ref[...].astype(BF16)
    for m in range(mb):
        xm = x_ref[:, m, :].astype(BF16)
        ar_out[m] = _dot(c1, xm)
        ai_out[m] = _dot(s1n, xm)


def _fft1(p4, c1, s1n, *, mb):
    bsz, n1, n2, _ = p4.shape
    mat = pl.BlockSpec((n1, n1), lambda b, j: (0, 0))
    out = pl.BlockSpec((None, mb, n1, BW), lambda b, j: (b, j, 0, 0))
    shape = jax.ShapeDtypeStruct((bsz, n2, n1, BW), F32)
    return pl.pallas_call(
        functools.partial(_fft1_kernel, mb=mb),
        grid=(bsz, n2 // mb),
        in_specs=[pl.BlockSpec((None, n1, mb, BW), lambda b, j: (b, 0, j, P_F // BW)),
                  mat, mat],
        out_specs=[out, out],
        out_shape=[shape, shape],
        compiler_params=_cparams(("parallel", "parallel")),
        name="fft_stage1",
    )(p4, c1, s1n)


def _fft2_kernel(ar_ref, ai_ref, f_ref, g_ref, o_ref, *, kb):
    g = g_ref[...].astype(BF16)
    for q in range(kb):
        ar = ar_ref[:, q, :]
        ai = ai_ref[:, q, :]
        rhs = jnp.concatenate([jnp.concatenate([ar, ai], axis=1),
                               jnp.concatenate([ai, -ar], axis=1)], axis=0).astype(BF16)
        uri = _dot(f_ref[q].astype(BF16), rhs)
        o_ref[:, q, :] = _dot(uri.astype(BF16), g)


def _fft2(ar, ai, ftab, gtab, *, kb):
    bsz, n2, n1, _ = ar.shape
    blk = pl.BlockSpec((None, n2, kb, BW), lambda b, j: (b, 0, j, 0))
    return pl.pallas_call(
        functools.partial(_fft2_kernel, kb=kb),
        grid=(bsz, n1 // kb),
        in_specs=[blk, blk,
                  pl.BlockSpec((kb, n2, 2 * n2), lambda b, j: (j, 0, 0)),
                  pl.BlockSpec((2 * BW, BW), lambda b, j: (0, 0))],
        out_specs=blk,
        out_shape=jax.ShapeDtypeStruct((bsz, n2, n1, BW), F32),
        compiler_params=_cparams(("parallel", "parallel")),
        name="fft_stage2",
    )(ar, ai, ftab, gtab)


def _dft_mats(n):
    idx = np.arange(n)
    ang = 2.0 * np.pi * ((idx[:, None] * idx[None, :]) % n) / n
    return np.cos(ang), np.sin(ang)


def _fourier_constants(s_len):
    n2 = FFT_N2
    n1 = s_len // n2
    c1, s1 = _dft_mats(n1)
    k1 = np.arange(n1)[:, None, None]
    k2 = np.arange(n2)[None, :, None]
    m2 = np.arange(n2)[None, None, :]
    ang = 2.0 * np.pi * ((m2 * (n1 * k2 + k1)) % s_len) / s_len
    ftab = np.concatenate([np.cos(ang), np.sin(ang)], axis=2)
    cg64, sg64 = _dft_mats(HD)
    norm = 1.0 / np.sqrt(float(s_len) * HD)
    gtab = np.concatenate([np.kron(np.eye(NH), cg64), np.kron(np.eye(NH), sg64)], axis=0) * norm
    f = lambda a: jnp.asarray(a, F32)
    return f(c1), f(-s1), f(ftab), f(gtab)


def _combine_kernel(x_ref, cv_ref, cvp_ref, cvn_ref, yf_ref, yb_ref, bonus_ref, rg_ref,
                    mla_ref, fn_ref, gn_ref, wg_ref, gb_ref, cw_ref, lng_ref, lnb_ref, avg_ref,
                    wa_ref, wb_ref, wc_ref, wd_ref, wo_ref, o_ref):
    i = pl.program_id(1)
    n_i = pl.num_programs(1)
    cv = cv_ref[...]
    cvp = cvp_ref[...]
    cvn = cvn_ref[...]
    z = cv[:, 2 * BW:] * cv[:, :BW]
    zp, zn = _shifted(z, cvp[:, 2 * BW:] * cvp[:, :BW], cvn[:, 2 * BW:] * cvn[:, :BW], i, n_i)
    conv = zp * cw_ref[0:1, :] + z * cw_ref[1:2, :] + zn * cw_ref[2:3, :]
    ya = _dot((cv[:, BW:2 * BW] * conv).astype(BF16), wa_ref[...])
    y = yf_ref[...] + yb_ref[...]
    avg = avg_ref[...]
    mean = _dot_exact_rhs(y, avg, 2)
    yc = y - mean
    var = _dot_exact_rhs(yc * yc, avg, 2)
    yn = yc * lax.rsqrt(var + RWKV_LN_EPS) * lng_ref[...] + lnb_ref[...]
    yb = _dot(((yn + bonus_ref[...]) * rg_ref[...]).astype(BF16), wb_ref[...])
    yc_ = _dot(mla_ref[...], wc_ref[...])
    yd = _dot(fn_ref[...].astype(BF16), wd_ref[...])
    x = x_ref[...]
    d = D_MODEL
    h = _rms(x, gn_ref[...]).astype(BF16)
    mix = None
    for br, yb_ in enumerate((ya, yb, yc_, yd)):
        cols = slice(br * d, (br + 1) * d)
        term = _sigmoid(_dot(h, wg_ref[:, cols]) + gb_ref[:, cols]) * yb_
        mix = term if mix is None else mix + term
    o_ref[...] = x + _dot(mix.astype(BF16), wo_ref[...])


def _combine(x3, p3, yf, yb, bonus, rg, mla_o, fn, gn, wg, gb, cw, lng, lnb, avg,
             wa, wb, wc, wd, wo, l, *, ts):
    bsz, s_len, d = x3.shape
    main, prev, nxt = _halo_specs(ts, 3 * BW, P_CONV // (3 * BW), s_len)
    row = lambda w: pl.BlockSpec((None, ts, w), lambda b, i: (b, i, 0))
    res = _resident
    lay = lambda shape: _layer(shape, l)
    return pl.pallas_call(
        _combine_kernel,
        grid=(bsz, s_len // ts),
        in_specs=[row(d), main, prev, nxt, row(BW), row(BW), row(BW), row(BW),
                  row(BW), row(BW),
                  res((1, d)), lay((d, N_BRANCH * d)), res((1, N_BRANCH * d)),
                  res((3, BW)), res((1, BW)), res((1, BW)), res((BW, BW)),
                  lay((BW, d)), lay((BW, d)), lay((BW, d)), lay((BW, d)),
                  lay((d, d))],
        out_specs=row(d),
        out_shape=jax.ShapeDtypeStruct(x3.shape, F32),
        compiler_params=_cparams(("parallel", "parallel")),
        name="combine",
    )(x3, p3, p3, p3, yf, yb, bonus, rg, mla_o, fn, gn, wg, gb, cw, lng, lnb, avg,
      wa, wb, wc, wd, wo)


def _ffn_kernel(x_ref, g_ref, wgu_ref, wd_ref, fg_ref, o_ref, *, final):
    x = x_ref[...]
    h = _rms(x, g_ref[...]).astype(BF16)
    gu = _dot(h, wgu_ref[...])
    gt = gu[:, :D_FF]
    act = (gt * _sigmoid(gt) * gu[:, D_FF:]).astype(BF16)
    y = x + _dot(act, wd_ref[...])
    if final:
        y = _rms(y, fg_ref[...])
    o_ref[...] = y


def _ffn(x, g, wgu, wd, fg, l, *, tm, final):
    n, d = x.shape
    return pl.pallas_call(
        functools.partial(_ffn_kernel, final=final),
        grid=(n // tm,),
        in_specs=[pl.BlockSpec((tm, d), lambda i: (i, 0)),
                  _resident((1, d)), _layer((d, 2 * D_FF), l), _layer((D_FF, d), l),
                  _resident((1, d))],
        out_specs=pl.BlockSpec((tm, d), lambda i: (i, 0)),
        out_shape=jax.ShapeDtypeStruct((n, d), F32),
        compiler_params=_cparams(("parallel",)),
        name="ffn",
    )(x, g, wgu, wd, fg)


def _head_cols(w, widths, total=128):
    k = w.shape[0]
    per = sum(widths)
    w = w.reshape(k, NH, per)
    w = jnp.pad(w, ((0, 0), (0, 0), (0, total - per)))
    return w.reshape(k, NH * total)


def _rope_swap_cols(w):
    k = w.shape[0]
    w = w.reshape(k, NH, 128)
    half = QK_ROPE // 2
    x1 = w[:, :, QK_NOPE:QK_NOPE + half]
    x2 = w[:, :, QK_NOPE + half:QK_NOPE + QK_ROPE]
    z = jnp.zeros_like(w)
    z = z.at[:, :, QK_NOPE:QK_NOPE + half].set(-x2)
    z = z.at[:, :, QK_NOPE + half:QK_NOPE + QK_ROPE].set(x1)
    return z.reshape(k, NH * 128)


def _pick(total, pref):
    t = min(total, pref)
    while total % t:
        t //= 2
    return t


def kernel(x, positions, mix_norm, w_in, gate_bias, conv_w, conv_out, rwkv_mu, rwkv_w0, rwkv_w_up, rwkv_a0, rwkv_a_up, rwkv_g_up, rwkv_k_k, rwkv_k_a, rwkv_r_k, rwkv_ln_g, rwkv_ln_b, rwkv_out, mla_q_norm, mla_w_uq, mla_kv_norm, mla_w_ukv, mla_out, fnet_out, w_o, ffn_norm, ffn_w_gu, ffn_w_down, final_norm):
    bsz, s_len, d = x.shape
    n = bsz * s_len
    depth = w_in.shape[0]
    n1 = s_len // FFT_N2

    inv_freq = ROPE_THETA ** (-jnp.arange(0, QK_ROPE, 2, dtype=F32) / QK_ROPE)
    pos8 = positions.astype(F32).reshape(bsz, s_len // 8, 8)
    ang = jnp.repeat(pos8, QK_ROPE // 2, axis=-1) * jnp.tile(inv_freq, 8)
    half = (bsz, s_len, QK_ROPE // 2)
    cos, sin = lax.optimization_barrier((jnp.cos(ang), jnp.sin(ang)))
    cos, sin = cos.reshape(half), sin.reshape(half)
    ones = jnp.ones((bsz, s_len, QK_NOPE), F32)
    zpad = jnp.zeros((bsz, s_len, 128 - QK_NOPE - QK_ROPE), F32)
    cs_tab = jnp.concatenate([ones, cos, cos, zpad], axis=-1).reshape(n, 128)
    sn_tab = jnp.concatenate([0 * ones, sin, sin, zpad], axis=-1).reshape(n, 128)

    c1, s1n, ftab, gtab = _fourier_constants(s_len)
    head_ones_np = np.kron(np.eye(NH), np.ones((HD, HD)))
    head_avg = jnp.asarray(head_ones_np / HD, BF16)
    head_mask = jnp.asarray(head_ones_np, BF16)

    place = np.zeros((128, NH * 128), np.float32)
    for h in range(NH):
        for jj in range(QK_ROPE):
            place[jj, h * 128 + QK_NOPE + jj] = 1.0
    place = jnp.asarray(place)
    place_sw = _rope_swap_cols(place)

    ts = _pick(s_len, 512)
    tq = _pick(s_len, 1024)
    cuts = np.cumsum([768, 768, 384, 256, 160, 256])

    zcols = lambda k: jnp.zeros((depth, d, k), BF16)
    wb16 = w_in.astype(BF16)
    w_small = jnp.concatenate(
        [wb16[:, :, :cuts[2]], zcols(P_Q - P_LORA - 384), wb16[:, :, cuts[2]:cuts[4]],
         zcols(256 - 160), wb16[:, :, cuts[4]:cuts[5]], zcols(P_W - P_F - 256)], axis=2)
    w_gate = wb16[:, :, cuts[5]:]
    wa_all, wb_all, wc_all, wd_all, wo_all = (
        t.astype(BF16) for t in (conv_out, rwkv_out, mla_out, fnet_out, w_o))
    wgu_all = ffn_w_gu.astype(BF16)
    wdn_all = ffn_w_down.astype(BF16)

    xf = x.reshape(n, d)
    for l in range(depth):
        g_mix = mix_norm[l].reshape(1, d)
        p2 = _norm_mm(xf, g_mix, w_small, l, tm=_pick(n, 512))
        p3 = p2.reshape(bsz, s_len, P_W)

        *scan_ops, gc, bonus, rg = _rwkv_prep(
            p3, rwkv_mu[l].reshape(2, 3 * BW), rwkv_w0[l], rwkv_w_up[l], rwkv_a0[l],
            rwkv_a_up[l], rwkv_g_up[l], rwkv_k_k[l].reshape(1, BW), rwkv_k_a[l].reshape(1, BW),
            rwkv_r_k[l].reshape(1, BW), head_mask, ts=ts)
        yf, yb = _rwkv_scan(scan_ops, gc, head_mask, tb=ts)

        wq = _head_cols(mla_w_uq[l], (QK_NOPE, QK_ROPE))
        wkv = mla_w_ukv[l].reshape(KV_LORA, NH, 2 * HD)
        wk = _head_cols(wkv[:, :, :HD].reshape(KV_LORA, NH * HD), (HD,))
        wv = _head_cols(wkv[:, :, HD:].reshape(KV_LORA, NH * HD), (HD,))
        q, k, v = _mla_prep(p2, cs_tab, sn_tab, mla_q_norm[l].reshape(1, Q_LORA),
                            wq.astype(BF16), _rope_swap_cols(wq).astype(BF16),
                            mla_kv_norm[l].reshape(1, KV_LORA), wk.astype(BF16),
                            wv.astype(BF16), place, place_sw, ts=ts)
        hw = NH * 128
        mla_o = _attention(q.reshape(bsz, s_len, hw), k.reshape(bsz, s_len, hw),
                           v.reshape(bsz, s_len, hw), tq=tq, tk=_pick(s_len, 4096))

        ar, ai = _fft1(p2.reshape(bsz, n1, FFT_N2, P_W), c1, s1n, mb=16)
        fn = _fft2(ar, ai, ftab, gtab, kb=_pick(n1, 8)).reshape(bsz, s_len, BW)

        x3 = _combine(xf.reshape(bsz, s_len, d), p3, yf, yb, bonus, rg, mla_o, fn,
                      g_mix, w_gate, gate_bias[l].reshape(1, N_BRANCH * d), conv_w[l],
                      rwkv_ln_g[l].reshape(1, BW), rwkv_ln_b[l].reshape(1, BW), head_avg,
                      wa_all, wb_all, wc_all, wd_all, wo_all, l, ts=ts)
        xf = _ffn(x3.reshape(n, d), ffn_norm[l].reshape(1, d), wgu_all, wdn_all,
                  final_norm.reshape(1, d), l, tm=_pick(n, 512), final=(l == depth - 1))
    return xf.reshape(bsz, s_len, d)
```

```python
import functools

import numpy as np
import jax
import jax.numpy as jnp
from jax import lax
from jax.experimental import pallas as pl
from jax.experimental.pallas import tpu as pltpu

F32 = jnp.float32
BF16 = jnp.bfloat16

D_MODEL = 1024
N_BRANCH = 4
BW = 256
HD = 64
NH = BW // HD
LORA = 64
GATE_LORA = 128
Q_LORA = 256
KV_LORA = 128
QK_NOPE = 64
QK_ROPE = 32
D_FF = 2816
NORM_EPS = 1e-6
RWKV_LN_EPS = 64e-5
ROPE_THETA = 10000.0
CHUNK = 64
SCAN_UNROLL = 4
FFT_N2 = 128

P_CONV = 0
P_RKV = 768
P_LORA = 1536
P_Q = 2048
P_KV = 2304
P_F = 2560
P_W = 2816

VMEM_LIMIT = 56 * 1024 * 1024


def _cparams(sem):
    return pltpu.CompilerParams(dimension_semantics=sem, vmem_limit_bytes=VMEM_LIMIT)


def _resident(shape):
    return pl.BlockSpec(shape, lambda *_: (0,) * len(shape), pipeline_mode=pl.Buffered(1))


def _layer(shape, l):
    return pl.BlockSpec((None,) + shape, lambda *_: (l,) + (0,) * len(shape),
                        pipeline_mode=pl.Buffered(1))


def _dot(a, b):
    return jnp.dot(a, b, preferred_element_type=F32)


def _dot_nt(a, b):
    return lax.dot_general(a, b, (((1,), (1,)), ((), ())), preferred_element_type=F32)


def _dot_tn(a, b):
    return lax.dot_general(a, b, (((0,), (0,)), ((), ())), preferred_element_type=F32)


def _dotb(a, b):
    return _dot(a.astype(BF16), b.astype(BF16))


def _bf16_terms(x, terms):
    parts = []
    for _ in range(terms):
        p = x.astype(BF16)
        parts.append(p)
        x = x - p.astype(F32)
    return parts


def _dot_exact_rhs(x, m01, terms):
    return sum(_dot(p, m01) for p in _bf16_terms(x, terms))


def _rms(xf, g, eps=NORM_EPS):
    return xf * lax.rsqrt(jnp.mean(xf * xf, axis=-1, keepdims=True) + eps) * g


def _sigmoid(z):
    return 0.5 + 0.5 * jnp.tanh(0.5 * z)


def _norm_mm_kernel(x_ref, g_ref, w_ref, o_ref):
    h = _rms(x_ref[...], g_ref[...]).astype(BF16)
    o_ref[...] = _dot(h, w_ref[...])


def _norm_mm(x, g, w, l, *, tm):
    n, d = x.shape
    nc = w.shape[2]
    return pl.pallas_call(
        _norm_mm_kernel,
        grid=(n // tm,),
        in_specs=[pl.BlockSpec((tm, d), lambda i: (i, 0)), _resident((1, d)),
                  _layer((d, nc), l)],
        out_specs=pl.BlockSpec((tm, nc), lambda i: (i, 0)),
        out_shape=jax.ShapeDtypeStruct((n, nc), F32),
        compiler_params=_cparams(("parallel",)),
        name="norm_mm",
    )(x, g, w)


def _shifted(t, prev_blk, next_blk, i, n_i):
    ts = t.shape[0]
    prev_row = jnp.where(i == 0, 0.0, prev_blk[7:8, :])
    next_row = jnp.where(i == n_i - 1, 0.0, next_blk[0:1, :])
    rows = lax.broadcasted_iota(jnp.int32, (ts, 1), 0)
    t_prev = jnp.where(rows == 0, prev_row, pltpu.roll(t, 1, axis=0))
    t_next = jnp.where(rows == ts - 1, next_row, pltpu.roll(t, ts - 1, axis=0))
    return t_prev, t_next


def _halo_specs(ts, width, col_blk, s_len):
    r8 = ts // 8
    last8 = s_len // 8 - 1
    main = pl.BlockSpec((None, ts, width), lambda b, i: (b, i, col_blk))
    prev = pl.BlockSpec((None, 8, width),
                        lambda b, i: (b, jnp.maximum(i * r8 - 1, 0), col_blk))
    nxt = pl.BlockSpec((None, 8, width),
                       lambda b, i: (b, jnp.minimum((i + 1) * r8, last8), col_blk))
    return main, prev, nxt


def _softplus(z):
    return jnp.maximum(z, 0.0) + jnp.log(1.0 + jnp.exp(-jnp.abs(z)))


def _rwkv_prep_kernel(rkv_ref, prev_ref, next_ref, lora_ref, mu_ref, w0_ref, wup_ref,
                      a0_ref, aup_ref, gup_ref, kk_ref, ka_ref, rk_ref, bd_ref,
                      at_out, rt_out, bt_out, kt_out, bh_out, kh_out, v_out, gc_out,
                      bonus_out, g_out):
    i = pl.program_id(1)
    n_i = pl.num_programs(1)
    t = rkv_ref[...]
    ts = t.shape[0]
    nck = ts // CHUNK
    t_prev, t_next = _shifted(t, prev_ref[...], next_ref[...], i, n_i)
    lora = lora_ref[...]
    bd = bd_ref[...]
    bonus = None
    per_dir = []
    for d in range(2):
        sh = t_prev if d == 0 else t_next
        mixed = t + mu_ref[d:d + 1, :] * (sh - t)
        rd = mixed[:, 0:BW]
        kd = mixed[:, BW:2 * BW]
        vd = mixed[:, 2 * BW:3 * BW]
        w_l = jnp.tanh(lora[:, d * LORA:(d + 1) * LORA])
        a_l = lora[:, 2 * LORA + d * LORA:2 * LORA + (d + 1) * LORA]
        w_pre = w0_ref[d:d + 1, :] + _dotb(w_l, wup_ref[d])
        w_log = -_softplus(-w_pre) - 0.5
        lw = -jnp.exp(w_log)
        a = _sigmoid(a0_ref[d:d + 1, :] + _dotb(a_l, aup_ref[d]))
        kk = kd * kk_ref[...]
        ss = _dot_exact_rhs(kk * kk, bd, 2)
        kk = kk / jnp.maximum(jnp.sqrt(ss), 1e-12)
        kt = kd * (1.0 + (a - 1.0) * ka_ref[...])
        bo = _dot_exact_rhs(rd * kt * rk_ref[...], bd, 2) * vd
        bonus = bo if bonus is None else bonus + bo
        per_dir.append((rd, kt, vd, kk, kk * a, lw))
    bonus_out[...] = bonus
    g_out[...] = _dotb(_sigmoid(lora[:, 4 * LORA:4 * LORA + GATE_LORA]), gup_ref[...])

    rows = lax.broadcasted_iota(jnp.int32, (ts, ts), 0)
    cols = lax.broadcasted_iota(jnp.int32, (ts, ts), 1)
    tri = jnp.where((rows // CHUNK == cols // CHUNK) & (cols <= rows), 1.0, 0.0).astype(BF16)
    lw_terms = _bf16_terms(jnp.concatenate([per_dir[0][5], per_dir[1][5]], axis=1), 3)
    pre = sum(_dot(tri, p) for p in lw_terms)
    for d in range(2):
        rd, kt, vd, kk, b, lw = per_dir[d]
        tot_c = jnp.sum(lw.reshape(nck, CHUNK, BW), axis=1)
        tot = jnp.broadcast_to(tot_c[:, None, :], (nck, CHUNK, BW)).reshape(ts, BW)
        p = pre[:, d * BW:(d + 1) * BW]
        cum = p if d == 0 else tot - p + lw
        g_inv = jnp.exp(-cum)
        g_tot_c = jnp.exp(tot_c)
        g_rem = jnp.broadcast_to(g_tot_c[:, None, :], (nck, CHUNK, BW)).reshape(ts, BW) * g_inv
        at_out[d] = (-kk * jnp.exp(cum - lw)).astype(BF16)
        rt_out[d] = (rd * jnp.exp(cum)).astype(BF16)
        bt_out[d] = (b * g_inv).astype(BF16)
        kt_out[d] = (kt * g_inv).astype(BF16)
        bh_out[d] = (b * g_rem).astype(BF16)
        kh_out[d] = (kt * g_rem).astype(BF16)
        v_out[d] = vd.astype(BF16)
        gc_out[d] = g_tot_c


def _rwkv_prep(p3, mu, w0, wup, a0, aup, gup, k_k, k_a, r_k, bd, *, ts):
    bsz, s_len, _ = p3.shape
    nck = ts // CHUNK
    main, prev, nxt = _halo_specs(ts, 3 * BW, P_RKV // (3 * BW), s_len)
    lora = pl.BlockSpec((None, ts, 384), lambda b, i: (b, i, P_LORA // 384))
    full = lambda shape: pl.BlockSpec(shape, lambda b, i: (0,) * len(shape))
    dir_out = pl.BlockSpec((2, None, ts, BW), lambda b, i: (0, b, i, 0))
    gc_out = pl.BlockSpec((2, None, nck, BW), lambda b, i: (0, b, i, 0))
    one_out = pl.BlockSpec((None, ts, BW), lambda b, i: (b, i, 0))
    dir_shape = jax.ShapeDtypeStruct((2, bsz, s_len, BW), BF16)
    gc_shape = jax.ShapeDtypeStruct((2, bsz, s_len // CHUNK, BW), F32)
    one_shape = jax.ShapeDtypeStruct((bsz, s_len, BW), F32)
    return pl.pallas_call(
        _rwkv_prep_kernel,
        grid=(bsz, s_len // ts),
        in_specs=[main, prev, nxt, lora,
                  full((2, 3 * BW)), full((2, BW)), full((2, LORA, BW)),
                  full((2, BW)), full((2, LORA, BW)), full((GATE_LORA, BW)),
                  full((1, BW)), full((1, BW)), full((1, BW)), full((BW, BW))],
        out_specs=[dir_out] * 7 + [gc_out] + [one_out] * 2,
        out_shape=[dir_shape] * 7 + [gc_shape] + [one_shape] * 2,
        compiler_params=_cparams(("parallel", "parallel")),
        name="rwkv_prep",
    )(p3, p3, p3, p3, mu, w0, wup, a0, aup, gup, k_k, k_a, r_k, bd)


def _rwkv_scan_kernel(*refs, n_chunk, bsz):
    n_in = 8
    fwd = refs[0:n_in]
    bwd = refs[n_in:2 * n_in]
    bdm_ref = refs[2 * n_in]
    yf_ref, yb_ref, h_ref = refs[2 * n_in + 1:]

    @pl.when(pl.program_id(0) == 0)
    def _():
        h_ref[...] = jnp.zeros_like(h_ref)

    bdm = bdm_ref[...]
    rows = lax.broadcasted_iota(jnp.int32, (CHUNK, BW), 0)
    cols = lax.broadcasted_iota(jnp.int32, (CHUNK, BW), 1) % CHUNK
    eye = jnp.where(rows == cols, 1.0, 0.0).astype(F32)

    def bd(x):
        xb = x.astype(BF16)
        return jnp.concatenate([xb] * NH, axis=0) * bdm

    def chunk_body(c2, carry):
        seqs = [(d, b) for d in range(2) for b in range(bsz)]
        nq = len(seqs)
        streams = [(par, d, b) for par in range(SCAN_UNROLL) for d, b in seqs]
        ns = len(streams)
        ld = []
        for par, d, b in streams:
            in_refs = fwd if d == 0 else bwd
            c = SCAN_UNROLL * c2 + par
            cidx = c if d == 0 else n_chunk - 1 - c
            sl = pl.ds(pl.multiple_of(cidx * CHUNK, CHUNK), CHUNK)
            ld.append([r[b, sl, :] for r in in_refs[:7]]
                      + [in_refs[7][b, pl.ds(cidx, 1), :], sl])
        strict = [(rows > cols) if d == 0 else (rows < cols) for _, d, _ in streams]
        incl = [(rows >= cols) if d == 0 else (rows <= cols) for _, d, _ in streams]
        m = [_dot_nt(jnp.concatenate([x[0], x[1]], axis=0),
                     jnp.concatenate([jnp.concatenate([x[2]] * NH, axis=0) * bdm,
                                      jnp.concatenate([x[3]] * NH, axis=0) * bdm], axis=0))
             for x in ld]
        a_ab = [jnp.where(strict[s], m[s][:CHUNK, :BW], 0.0) for s in range(ns)]
        a_ak = [jnp.where(strict[s], m[s][:CHUNK, BW:], 0.0) for s in range(ns)]
        a_rb = [jnp.where(incl[s], m[s][CHUNK:, :BW], 0.0).astype(BF16) for s in range(ns)]
        a_rk = [jnp.where(incl[s], m[s][CHUNK:, BW:], 0.0) for s in range(ns)]
        avs = [_dot(jnp.concatenate([a_ak[s], a_rk[s]], axis=0).astype(BF16), bd(ld[s][6]))
               for s in range(ns)]
        tm = [eye + a for a in a_ab]
        pw = [_dot(a.astype(BF16), bd(a)) for a in a_ab]
        for _ in range(4):
            res = [_dot(jnp.concatenate([tm[s], pw[s]], axis=0).astype(BF16), bd(pw[s]))
                   for s in range(ns)]
            tm = [tm[s] + res[s][:CHUNK] for s in range(ns)]
            pw = [r[CHUNK:] for r in res]
        tm = [(tm[s] + _dot(tm[s].astype(BF16), bd(pw[s]))).astype(BF16) for s in range(ns)]
        p = [_dot(tm[s], bd(ld[s][0])) for s in range(ns)]
        q = [_dot(tm[s], bd(avs[s][:CHUNK])) for s in range(ns)]
        g = [ld[s][1].astype(F32) + _dot(a_rb[s], bd(p[s])) for s in range(ns)]
        y0 = [avs[s][CHUNK:] + _dot(a_rb[s], bd(q[s])) for s in range(ns)]
        gpl = [jnp.concatenate([g[s], p[s]], axis=0).astype(BF16) for s in range(ns)]
        hst = [h_ref[i] for i in range(nq)]
        for par in range(SCAN_UNROLL):
            st =[par * nq + i for i in range(nq)]
            gp = [_dot_nt(gpl[s], hst[i].astype(BF16)) for i, s in enumerate(st)]
            for i, s in enumerate(st):
                _, d, b = streams[s]
                y_ref = yf_ref if d == 0 else yb_ref
                y_ref[b, ld[s][8], :] = gp[i][:CHUNK] + y0[s]
            upd = [_dot_tn(jnp.concatenate([(gp[i][CHUNK:] + q[s]).astype(BF16), ld[s][6]],
                                           axis=0),
                           jnp.concatenate([ld[s][4], ld[s][5]], axis=0))
                   for i, s in enumerate(st)]
            hst = [hst[i] * ld[s][7] + upd[i] * bdm.astype(F32) for i, s in enumerate(st)]
        for i in range(nq):
            h_ref[i] = hst[i]
        return carry

    lax.fori_loop(0, n_chunk // SCAN_UNROLL, chunk_body, 0)


def _rwkv_scan(ops, gc, bdm, *, tb):
    _, bsz, s_len, _ = ops[0].shape
    nb = s_len // tb
    nck = tb // CHUNK

    def specs(d):
        blk = (lambda i: i) if d == 0 else (lambda i: nb - 1 - i)
        big = pl.BlockSpec((None, bsz, tb, BW), lambda i: (d, 0, blk(i), 0))
        small = pl.BlockSpec((None, bsz, nck, BW), lambda i: (d, 0, blk(i), 0))
        return [big] * 7 + [small]

    out_f = pl.BlockSpec((bsz, tb, BW), lambda i: (0, i, 0))
    out_b = pl.BlockSpec((bsz, tb, BW), lambda i: (0, nb - 1 - i, 0))
    shape = jax.ShapeDtypeStruct((bsz, s_len, BW), F32)
    args = list(ops) + [gc]
    return pl.pallas_call(
        functools.partial(_rwkv_scan_kernel, n_chunk=nck, bsz=bsz),
        grid=(nb,),
        in_specs=specs(0) + specs(1) + [pl.BlockSpec((BW, BW), lambda i: (0, 0))],
        out_specs=[out_f, out_b],
        out_shape=[shape, shape],
        scratch_shapes=[pltpu.VMEM((2 * bsz, BW, BW), F32)],
        compiler_params=_cparams(("arbitrary",)),
        name="rwkv_scan",
    )(*args, *args, bdm)


def _mla_prep_kernel(qlo_ref, kvlo_ref, cs_ref, sn_ref, qn_ref, wq_ref, wqs_ref,
                     kvn_ref, wk_ref, wv_ref, pl_ref, pls_ref, q_out, k_out, v_out, *, scale):
    cs = jnp.concatenate([cs_ref[...]] * NH, axis=1)
    sn = jnp.concatenate([sn_ref[...]] * NH, axis=1)
    hq = _rms(qlo_ref[...], qn_ref[...]).astype(BF16)
    q = _dot(hq, wq_ref[...]) * cs + _dot(hq, wqs_ref[...]) * sn
    q_out[...] = (q * scale).astype(BF16)
    kv = kvlo_ref[...]
    hkv = _rms(kv[:, :KV_LORA], kvn_ref[...]).astype(BF16)
    kr = kv[:, KV_LORA:]
    k = (_dot(hkv, wk_ref[...]) + _dotb(kr, pl_ref[...])) * cs + _dotb(kr, pls_ref[...]) * sn
    k_out[...] = k.astype(BF16)
    v = _dot(hkv, wv_ref[...])
    lane = lax.broadcasted_iota(jnp.int32, v.shape, 1) % 128
    v_out[...] = jnp.where(lane == HD, 1.0, v).astype(BF16)


def _mla_prep(p2, cs, sn, qn, wq, wqs, kvn, wk, wv, plc, pls, *, ts):
    n = p2.shape[0]
    full = lambda shape: pl.BlockSpec(shape, lambda i: (0,) * len(shape))
    hw = NH * 128
    out = pl.BlockSpec((ts, hw), lambda i: (i, 0))
    shape = jax.ShapeDtypeStruct((n, hw), BF16)
    return pl.pallas_call(
        functools.partial(_mla_prep_kernel, scale=float((QK_NOPE + QK_ROPE) ** -0.5 * np.log2(np.e))),
        grid=(n // ts,),
        in_specs=[pl.BlockSpec((ts, 256), lambda i: (i, P_Q // 256)),
                  pl.BlockSpec((ts, 256), lambda i: (i, P_KV // 256)),
                  pl.BlockSpec((ts, 128), lambda i: (i, 0)),
                  pl.BlockSpec((ts, 128), lambda i: (i, 0)),
                  full((1, Q_LORA)), full((Q_LORA, hw)), full((Q_LORA, hw)),
                  full((1, KV_LORA)), full((KV_LORA, hw)), full((KV_LORA, hw)),
                  full((128, hw)), full((128, hw))],
        out_specs=[out] * 3,
        out_shape=[shape] * 3,
        compiler_params=_cparams(("parallel",)),
        name="mla_prep",
    )(p2, p2, cs, sn, qn, wq, wqs, kvn, wk, wv, plc, pls)


def _attn_kernel(q_ref, k_ref, v_ref, o_ref, m_ref, acc_ref, *, rb):
    j = pl.program_id(2)

    @pl.when(j == 0)
    def _():
        m_ref[...] = jnp.full_like(m_ref, -jnp.inf)
        acc_ref[...] = jnp.zeros_like(acc_ref)

    units = [(h, r) for h in range(NH) for r in range(q_ref.shape[0] // rb)]

    def scores(h, r):
        hs = slice(h * 128, (h + 1) * 128)
        return _dot_nt(q_ref[r * rb:(r + 1) * rb, hs], k_ref[:, hs])

    s_next = scores(*units[0])
    for idx, (h, r) in enumerate(units):
        s = s_next
        if idx + 1 < len(units):
            s_next = scores(*units[idx + 1])
        rows = slice(r * rb, (r + 1) * rb)
        m_old = m_ref[h, rows]
        m_new = jnp.maximum(m_old, jnp.max(s, axis=-1, keepdims=True))
        p = jnp.exp2(s - m_new).astype(BF16)
        alpha = jnp.exp2(m_old - m_new)
        acc_ref[h, rows] = alpha * acc_ref[h, rows] + _dot(p, v_ref[:, h * 128:(h + 1) * 128])
        m_ref[h, rows] = m_new

    @pl.when(j == pl.num_programs(2) - 1)
    def _():
        outs = []
        for h in range(NH):
            acc = acc_ref[h]
            outs.append(acc[:, :HD] / acc[:, HD:HD + 1])
        o_ref[...] = jnp.concatenate(outs, axis=1).astype(o_ref.dtype)


def _attention(q, k, v, *, tq, tk):
    bsz, s_len, hw = q.shape
    return pl.pallas_call(
        functools.partial(_attn_kernel, rb=_pick(tq, 256)),
        grid=(bsz, s_len // tq, s_len // tk),
        in_specs=[pl.BlockSpec((None, tq, hw), lambda b, i, j: (b, i, 0)),
                  pl.BlockSpec((None, tk, hw), lambda b, i, j: (b, j, 0)),
                  pl.BlockSpec((None, tk, hw), lambda b, i, j: (b, j, 0))],
        out_specs=pl.BlockSpec((None, tq, BW), lambda b, i, j: (b, i, 0)),
        out_shape=jax.ShapeDtypeStruct((bsz, s_len, BW), BF16),
        scratch_shapes=[pltpu.VMEM((NH, tq, 1), F32), pltpu.VMEM((NH, tq, 128), F32)],
        compiler_params=_cparams(("parallel", "parallel", "arbitrary")),
        name="mla_attention",
    )(q, k, v)


def _fft1_kernel(x_ref, c1_ref, s1n_ref, ar_out, ai_out, *, mb):
    c1 = c1_ref[...].astype(BF16)
    s1n = s1n_ref[...].astype(BF16)
    for m in range(mb):
        xm = x_ref[:, m, :].astype(BF16)
        ar_out[m] = _dot(c1, xm)
        ai_out[m] = _dot(s1n, xm)


def _fft1(p4, c1, s1n, *, mb):
    bsz, n1, n2, _ = p4.shape
    mat = pl.BlockSpec((n1, n1), lambda b, j: (0, 0))
    out = pl.BlockSpec((None, mb, n1, BW), lambda b, j: (b, j, 0, 0))
    shape = jax.ShapeDtypeStruct((bsz, n2, n1, BW), F32)
    return pl.pallas_call(
        functools.partial(_fft1_kernel, mb=mb),
        grid=(bsz, n2 // mb),
        in_specs=[pl.BlockSpec((None, n1, mb, BW), lambda b, j: (b, 0, j, P_F // BW)),
                  mat, mat],
        out_specs=[out, out],
        out_shape=[shape, shape],
        compiler_params=_cparams(("parallel", "parallel")),
        name="fft_stage1",
    )(p4, c1, s1n)


def _fft2_kernel(ar_ref, ai_ref, f_ref, g_ref, o_ref, *, kb):
    g = g_ref[...].astype(BF16)
    for q in range(kb):
        ar = ar_ref[:, q, :]
        ai = ai_ref[:, q, :]
        rhs = jnp.concatenate([jnp.concatenate([ar, ai], axis=1),
                               jnp.concatenate([ai, -ar], axis=1)], axis=0).astype(BF16)
        uri = _dot(f_ref[q].astype(BF16), rhs)
        o_ref[:, q, :] = _dot(uri.astype(BF16), g)


def _fft2(ar, ai, ftab, gtab, *, kb):
    bsz, n2, n1, _ = ar.shape
    blk = pl.BlockSpec((None, n2, kb, BW), lambda b, j: (b, 0, j, 0))
    return pl.pallas_call(
        functools.partial(_fft2_kernel, kb=kb),
        grid=(bsz, n1 // kb),
        in_specs=[blk, blk,
                  pl.BlockSpec((kb, n2, 2 * n2), lambda b, j: (j, 0, 0)),
                  pl.BlockSpec((2 * BW, BW), lambda b, j: (0, 0))],
        out_specs=blk,
        out_shape=jax.ShapeDtypeStruct((bsz, n2, n1, BW), F32),
        compiler_params=_cparams(("parallel", "parallel")),
        name="fft_stage2",
    )(ar, ai, ftab, gtab)


def _dft_mats(n):
    idx = np.arange(n)
    ang = 2.0 * np.pi * ((idx[:, None] * idx[None, :]) % n) / n
    return np.cos(ang), np.sin(ang)


def _fourier_constants(s_len):
    n2 = FFT_N2
    n1 = s_len // n2
    c1, s1 = _dft_mats(n1)
    k1 = np.arange(n1)[:, None, None]
    k2 = np.arange(n2)[None, :, None]
    m2 = np.arange(n2)[None, None, :]
    ang = 2.0 * np.pi * ((m2 * (n1 * k2 + k1)) % s_len) / s_len
    ftab = np.concatenate([np.cos(ang), np.sin(ang)], axis=2)
    cg64, sg64 = _dft_mats(HD)
    norm = 1.0 / np.sqrt(float(s_len) * HD)
    gtab = np.concatenate([np.kron(np.eye(NH), cg64), np.kron(np.eye(NH), sg64)], axis=0) * norm
    f = lambda a: jnp.asarray(a, F32)
    return f(c1), f(-s1), f(ftab), f(gtab)


def _combine_kernel(x_ref, cv_ref, cvp_ref, cvn_ref, yf_ref, yb_ref, bonus_ref, rg_ref,
                    mla_ref, fn_ref, gn_ref, wg_ref, gb_ref, cw_ref, lng_ref, lnb_ref, avg_ref,
                    wa_ref, wb_ref, wc_ref, wd_ref, wo_ref, o_ref):
    i = pl.program_id(1)
    n_i = pl.num_programs(1)
    cv = cv_ref[...]
    cvp = cvp_ref[...]
    cvn = cvn_ref[...]
    z = cv[:, 2 * BW:] * cv[:, :BW]
    zp, zn = _shifted(z, cvp[:, 2 * BW:] * cvp[:, :BW], cvn[:, 2 * BW:] * cvn[:, :BW], i, n_i)
    conv = zp * cw_ref[0:1, :] + z * cw_ref[1:2, :] + zn * cw_ref[2:3, :]
    ya = _dot((cv[:, BW:2 * BW] * conv).astype(BF16), wa_ref[...])
    y = yf_ref[...] + yb_ref[...]
    avg = avg_ref[...]
    mean = _dot_exact_rhs(y, avg, 2)
    yc = y - mean
    var = _dot_exact_rhs(yc * yc, avg, 2)
    yn = yc * lax.rsqrt(var + RWKV_LN_EPS) * lng_ref[...] + lnb_ref[...]
    yb = _dot(((yn + bonus_ref[...]) * rg_ref[...]).astype(BF16), wb_ref[...])
    yc_ = _dot(mla_ref[...], wc_ref[...])
    yd = _dot(fn_ref[...].astype(BF16), wd_ref[...])
    x = x_ref[...]
    d = D_MODEL
    h = _rms(x, gn_ref[...]).astype(BF16)
    mix = None
    for br, yb_ in enumerate((ya, yb, yc_, yd)):
        cols = slice(br * d, (br + 1) * d)
        term = _sigmoid(_dot(h, wg_ref[:, cols]) + gb_ref[:, cols]) * yb_
        mix = term if mix is None else mix + term
    o_ref[...] = x + _dot(mix.astype(BF16), wo_ref[...])


def _combine(x3, p3, yf, yb, bonus, rg, mla_o, fn, gn, wg, gb, cw, lng, lnb, avg,
             wa, wb, wc, wd, wo, l, *, ts):
    bsz, s_len, d = x3.shape
    main, prev, nxt = _halo_specs(ts, 3 * BW, P_CONV // (3 * BW), s_len)
    row = lambda w: pl.BlockSpec((None, ts, w), lambda b, i: (b, i, 0))
    res = _resident
    lay = lambda shape: _layer(shape, l)
    return pl.pallas_call(
        _combine_kernel,
        grid=(bsz, s_len // ts),
        in_specs=[row(d), main, prev, nxt, row(BW), row(BW), row(BW), row(BW),
                  row(BW), row(BW),
                  res((1, d)), lay((d, N_BRANCH * d)), res((1, N_BRANCH * d)),
                  res((3, BW)), res((1, BW)), res((1, BW)), res((BW, BW)),
                  lay((BW, d)), lay((BW, d)), lay((BW, d)), lay((BW, d)),
                  lay((d, d))],
        out_specs=row(d),
        out_shape=jax.ShapeDtypeStruct(x3.shape, F32),
        compiler_params=_cparams(("parallel", "parallel")),
        name="combine",
    )(x3, p3, p3, p3, yf, yb, bonus, rg, mla_o, fn, gn, wg, gb, cw, lng, lnb, avg,
      wa, wb, wc, wd, wo)


def _ffn_kernel(x_ref, g_ref, wgu_ref, wd_ref, fg_ref, o_ref, *, final):
    x = x_ref[...]
    h = _rms(x, g_ref[...]).astype(BF16)
    gu = _dot(h, wgu_ref[...])
    gt = gu[:, :D_FF]
    act = (gt * _sigmoid(gt) * gu[:, D_FF:]).astype(BF16)
    y = x + _dot(act, wd_ref[...])
    if final:
        y = _rms(y, fg_ref[...])
    o_ref[...] = y


def _ffn(x, g, wgu, wd, fg, l, *, tm, final):
    n, d = x.shape
    return pl.pallas_call(
        functools.partial(_ffn_kernel, final=final),
        grid=(n // tm,),
        in_specs=[pl.BlockSpec((tm, d), lambda i: (i, 0)),
                  _resident((1, d)), _layer((d, 2 * D_FF), l), _layer((D_FF, d), l),
                  _resident((1, d))],
        out_specs=pl.BlockSpec((tm, d), lambda i: (i, 0)),
        out_shape=jax.ShapeDtypeStruct((n, d), F32),
        compiler_params=_cparams(("parallel",)),
        name="ffn",
    )(x, g, wgu, wd, fg)


def _head_cols(w, widths, total=128):
    k = w.shape[0]
    per = sum(widths)
    w = w.reshape(k, NH, per)
    w = jnp.pad(w, ((0, 0), (0, 0), (0, total - per)))
    return w.reshape(k, NH * total)


def _rope_swap_cols(w):
    k = w.shape[0]
    w = w.reshape(k, NH, 128)
    half = QK_ROPE // 2
    x1 = w[:, :, QK_NOPE:QK_NOPE + half]
    x2 = w[:, :, QK_NOPE + half:QK_NOPE + QK_ROPE]
    z = jnp.zeros_like(w)
    z = z.at[:, :, QK_NOPE:QK_NOPE + half].set(-x2)
    z = z.at[:, :, QK_NOPE + half:QK_NOPE + QK_ROPE].set(x1)
    return z.reshape(k, NH * 128)


def _pick(total, pref):
    t = min(total, pref)
    while total % t:
        t //= 2
    return t


def kernel(x, positions, mix_norm, w_in, gate_bias, conv_w, conv_out, rwkv_mu, rwkv_w0, rwkv_w_up, rwkv_a0, rwkv_a_up, rwkv_g_up, rwkv_k_k, rwkv_k_a, rwkv_r_k, rwkv_ln_g, rwkv_ln_b, rwkv_out, mla_q_norm, mla_w_uq, mla_kv_norm, mla_w_ukv, mla_out, fnet_out, w_o, ffn_norm, ffn_w_gu, ffn_w_down, final_norm):
    bsz, s_len, d = x.shape
    n = bsz * s_len
    depth = w_in.shape[0]
    n1 = s_len // FFT_N2

    inv_freq = ROPE_THETA ** (-jnp.arange(0, QK_ROPE, 2, dtype=F32) / QK_ROPE)
    pos8 = positions.astype(F32).reshape(bsz, s_len // 8, 8)
    ang = jnp.repeat(pos8, QK_ROPE // 2, axis=-1) * jnp.tile(inv_freq, 8)
    half = (bsz, s_len, QK_ROPE // 2)
    cos, sin = lax.optimization_barrier((jnp.cos(ang), jnp.sin(ang)))
    cos, sin = cos.reshape(half), sin.reshape(half)
    ones = jnp.ones((bsz, s_len, QK_NOPE), F32)
    zpad = jnp.zeros((bsz, s_len, 128 - QK_NOPE - QK_ROPE), F32)
    cs_tab = jnp.concatenate([ones, cos, cos, zpad], axis=-1).reshape(n, 128)
    sn_tab = jnp.concatenate([0 * ones, sin, sin, zpad], axis=-1).reshape(n, 128)

    c1, s1n, ftab, gtab = _fourier_constants(s_len)
    head_ones_np = np.kron(np.eye(NH), np.ones((HD, HD)))
    head_avg = jnp.asarray(head_ones_np / HD, BF16)
    head_mask = jnp.asarray(head_ones_np, BF16)

    place = np.zeros((128, NH * 128), np.float32)
    for h in range(NH):
        for jj in range(QK_ROPE):
            place[jj, h * 128 + QK_NOPE + jj] = 1.0
    place = jnp.asarray(place)
    place_sw = _rope_swap_cols(place)

    ts = _pick(s_len, 512)
    tq = _pick(s_len, 1024)
    cuts = np.cumsum([768, 768, 384, 256, 160, 256])

    zcols = lambda k: jnp.zeros((depth, d, k), BF16)
    wb16 = w_in.astype(BF16)
    w_small = jnp.concatenate(
        [wb16[:, :, :cuts[2]], zcols(P_Q - P_LORA - 384), wb16[:, :, cuts[2]:cuts[4]],
         zcols(256 - 160), wb16[:, :, cuts[4]:cuts[5]], zcols(P_W - P_F - 256)], axis=2)
    w_gate = wb16[:, :, cuts[5]:]
    wa_all, wb_all, wc_all, wd_all, wo_all = (
        t.astype(BF16) for t in (conv_out, rwkv_out, mla_out, fnet_out, w_o))
    wgu_all = ffn_w_gu.astype(BF16)
    wdn_all = ffn_w_down.astype(BF16)

    xf = x.reshape(n, d)
    for l in range(depth):
        g_mix = mix_norm[l].reshape(1, d)
        p2 = _norm_mm(xf, g_mix, w_small, l, tm=_pick(n, 512))
        p3 = p2.reshape(bsz, s_len, P_W)

        *scan_ops, gc, bonus, rg = _rwkv_prep(
            p3, rwkv_mu[l].reshape(2, 3 * BW), rwkv_w0[l], rwkv_w_up[l], rwkv_a0[l],
            rwkv_a_up[l], rwkv_g_up[l], rwkv_k_k[l].reshape(1, BW), rwkv_k_a[l].reshape(1, BW),
            rwkv_r_k[l].reshape(1, BW), head_mask, ts=ts)
        yf, yb = _rwkv_scan(scan_ops, gc, head_mask, tb=ts)

        wq = _head_cols(mla_w_uq[l], (QK_NOPE, QK_ROPE))
        wkv = mla_w_ukv[l].reshape(KV_LORA, NH, 2 * HD)
        wk = _head_cols(wkv[:, :, :HD].reshape(KV_LORA, NH * HD), (HD,))
        wv = _head_cols(wkv[:, :, HD:].reshape(KV_LORA, NH * HD), (HD,))
        q, k, v = _mla_prep(p2, cs_tab, sn_tab, mla_q_norm[l].reshape(1, Q_LORA),
                            wq.astype(BF16), _rope_swap_cols(wq).astype(BF16),
                            mla_kv_norm[l].reshape(1, KV_LORA), wk.astype(BF16),
                            wv.astype(BF16), place, place_sw, ts=ts)
        hw = NH * 128
        mla_o = _attention(q.reshape(bsz, s_len, hw), k.reshape(bsz, s_len, hw),
                           v.reshape(bsz, s_len, hw), tq=tq, tk=_pick(s_len, 4096))

        ar, ai = _fft1(p2.reshape(bsz, n1, FFT_N2, P_W), c1, s1n, mb=16)
        fn = _fft2(ar, ai, ftab, gtab, kb=_pick(n1, 8)).reshape(bsz, s_len, BW)

        x3 = _combine(xf.reshape(bsz, s_len, d), p3, yf, yb, bonus, rg, mla_o, fn,
                      g_mix, w_gate, gate_bias[l].reshape(1, N_BRANCH * d), conv_w[l],
                      rwkv_ln_g[l].reshape(1, BW), rwkv_ln_b[l].reshape(1, BW), head_avg,
                      wa_all, wb_all, wc_all, wd_all, wo_all, l, ts=ts)
        xf = _ffn(x3.reshape(n, d), ffn_norm[l].reshape(1, d), wgu_all, wdn_all,
                  final_norm.reshape(1, d), l, tm=_pick(n, 512), final=(l == depth - 1))
    return xf.reshape(bsz, s_len, d)
```
